```python
import math
import jax, jax.numpy as jnp
from jax import lax
import numpy as np

D_MODEL = 1024
BATCH = 8
SEQ = 4096
DEPTH = 1

EPS = 1e-6
HG_HEADS = 4
HG_DK = 128
HG_DV = 128
HG_CHUNK = 64
HG_W = HG_HEADS * HG_DK
HG_VW = HG_HEADS * HG_DV
ATT_HEADS = 8
ATT_DH = 64
ATT_W = ATT_HEADS * ATT_DH
IDX_HEADS = 4
IDX_DH = 64
TOPK_MAX = 256
Q_BLOCK = 128
REL_BUCKETS = 32
REL_MAX_DIST = 128
PEER_HEADS = 8
PEER_NKEYS = 128
PEER_EXPERTS = PEER_NKEYS * PEER_NKEYS
PEER_QDIM = 256
PEER_HALF = PEER_QDIM // 2
PEER_TOPK = 16
PEER_TOKEN_BLOCK = 128
IN_SIZES = (HG_W, HG_W, HG_VW, HG_VW, ATT_W, ATT_DH, ATT_DH, IDX_HEADS * IDX_DH, IDX_DH, IDX_HEADS, D_MODEL, D_MODEL)
IN_WIDTH = sum(IN_SIZES)

kernel_name = "hybrid_hgrn2_dsa_peer_block"


def rmsnorm(x, g):
    xf = x.astype(jnp.float32)
    y = xf * lax.rsqrt(jnp.mean(xf * xf, axis=-1, keepdims=True) + EPS)
    return (y * g.astype(jnp.float32)).astype(x.dtype)


def layernorm(x, g, b):
    xf = x.astype(jnp.float32)
    mu = jnp.mean(xf, axis=-1, keepdims=True)
    var = jnp.mean(jnp.square(xf - mu), axis=-1, keepdims=True)
    y = (xf - mu) * lax.rsqrt(var + EPS)
    return (y * g.astype(jnp.float32) + b.astype(jnp.float32)).astype(x.dtype)


def split_cols(p, sizes):
    offs = np.cumsum(np.array(sizes))[:-1].tolist()
    return jnp.split(p, offs, axis=-1)


def t5_bucket(dist):
    n = jnp.maximum(dist, 0)
    max_exact = REL_BUCKETS // 2
    nf = jnp.maximum(n, 1).astype(jnp.float32)
    large = max_exact + (jnp.log(nf / max_exact) / math.log(REL_MAX_DIST / max_exact)
                         * (REL_BUCKETS - max_exact)).astype(jnp.int32)
    large = jnp.minimum(large, REL_BUCKETS - 1)
    return jnp.where(n < max_exact, n, large)


def hgrn2_mixer(q_raw, f_raw, i_raw, og_raw, lb, gn):
    B, S, _ = q_raw.shape
    H, DK, DV, C = HG_HEADS, HG_DK, HG_DV, HG_CHUNK
    n = S // C
    f32 = jnp.float32
    q = jax.nn.silu(q_raw.astype(f32))
    fg = lb + (1.0 - lb) * jax.nn.sigmoid(f_raw.astype(f32))
    k = 1.0 - fg
    logf = jnp.log(fg)
    v = i_raw.astype(f32)

    def to_chunks(a, d):
        return a.reshape(B, n, C, H, d).transpose(1, 0, 3, 2, 4)

    qc, kc, lc, vc = to_chunks(q, DK), to_chunks(k, DK), to_chunks(logf, DK), to_chunks(v, DV)
    tri = jnp.tril(jnp.ones((C, C), dtype=bool))[None, None, :, :, None]

    def step(state, inp):
        qb, kb, lfb, vb = inp
        b = jnp.cumsum(lfb, axis=2)
        inter = jnp.einsum('bhtd,bhde->bhte', qb * jnp.exp(b), state)
        diff = b[:, :, :, None, :] - b[:, :, None, :, :]
        decay = jnp.where(tri, jnp.exp(jnp.where(tri, diff, 0.0)), 0.0)
        A = jnp.einsum('bhtd,bhsd,bhtsd->bhts', qb, kb, decay)
        intra = jnp.einsum('bhts,bhse->bhte', A, vb)
        b_last = b[:, :, -1:, :]
        new_state = (jnp.exp(b_last[:, :, 0, :])[..., None] * state
                     + jnp.einsum('bhsd,bhse->bhde', kb * jnp.exp(b_last - b), vb))
        return new_state, inter + intra

    s0 = jnp.zeros((B, H, DK, DV), f32)
    _, o = lax.scan(step, s0, (qc, kc, lc, vc))
    o = o.transpose(1, 0, 3, 2, 4).reshape(B, S, H, DV)
    o = o * lax.rsqrt(jnp.mean(o * o, axis=-1, keepdims=True) + EPS) * gn.astype(f32)
    o = o.reshape(B, S, H * DV) * jax.nn.silu(og_raw.astype(f32))
    return o.astype(q_raw.dtype)


def dsa_mixer(q_raw, k_raw, v_raw, iq_raw, ik_raw, iw_raw, ik_g, ik_b, rel_table):
    B, S, _ = q_raw.shape
    f32 = jnp.float32
    nblk = S // Q_BLOCK
    ktop = min(TOPK_MAX, S // 4)
    ik = layernorm(ik_raw, ik_g, ik_b).astype(f32)
    iw = iw_raw.astype(f32) * (IDX_HEADS ** -0.5)
    qs = q_raw.reshape(B, nblk, Q_BLOCK, ATT_HEADS, ATT_DH).transpose(1, 0, 2, 3, 4)
    iqs = iq_raw.reshape(B, nblk, Q_BLOCK, IDX_HEADS, IDX_DH).transpose(1, 0, 2, 3, 4)
    iws = iw.reshape(B, nblk, Q_BLOCK, IDX_HEADS).transpose(1, 0, 2, 3)
    spos = jnp.arange(S)
    gather = jax.vmap(lambda tab, idx: tab[idx])

    def block(args):
        bi, qb, iqb, iwb = args
        tpos = bi * Q_BLOCK + jnp.arange(Q_BLOCK)
        rel = jnp.einsum('bqjd,bsd->bqjs', iqb.astype(f32), ik) * (IDX_DH ** -0.5)
        score = jnp.einsum('bqj,bqjs->bqs', iwb, jax.nn.relu(rel))
        causal = spos[None, :] <= tpos[:, None]
        score = jnp.where(causal[None], score, -jnp.inf)
        top_score, idx = lax.top_k(score, ktop)
        valid = jnp.isfinite(top_score)
        kg = gather(k_raw, idx)
        vg = gather(v_raw, idx)
        bias = rel_table[t5_bucket(tpos[None, :, None] - idx)].astype(f32)
        logits = (jnp.einsum('bqhd,bqkd->bqhk', qb, kg).astype(f32) * (ATT_DH ** -0.5)
                  + bias.transpose(0, 1, 3, 2))
        logits = jnp.where(valid[:, :, None, :], logits, -jnp.inf)
        p = jax.nn.softmax(logits, axis=-1)
        return jnp.einsum('bqhk,bqkd->bqhd', p.astype(vg.dtype), vg)

    o = lax.map(block, (jnp.arange(nblk), qs, iqs, iws))
    return o.transpose(1, 0, 2, 3, 4).reshape(B, S, ATT_W)


def peer_ffn(x, w_q, sub_keys, u, v):
    B, S, D = x.shape
    TB, H, K = PEER_TOKEN_BLOCK, PEER_HEADS, PEER_TOPK
    xt = x.reshape(B * S // TB, TB, D)

    def block(xb):
        q = (xb @ w_q).reshape(TB, H, 2, PEER_HALF)
        s = jnp.einsum('thpc,phnc->thpn', q, sub_keys).astype(jnp.float32)
        s1, i1 = lax.top_k(s[:, :, 0], K)
        s2, i2 = lax.top_k(s[:, :, 1], K)
        cand = (s1[..., :, None] + s2[..., None, :]).reshape(TB, H, K * K)
        cidx = (i1[..., :, None] * PEER_NKEYS + i2[..., None, :]).reshape(TB, H, K * K)
        top_s, pos = lax.top_k(cand, K)
        eidx = jnp.take_along_axis(cidx, pos, axis=-1).reshape(TB, H * K)
        g = jax.nn.softmax(top_s, axis=-1).reshape(TB, H * K)
        ug = u[eidx]
        vg = v[eidx]
        h = jnp.einsum('tkd,td->tk', ug, xb)
        a = jax.nn.gelu(h, approximate=False) * g.astype(h.dtype)
        return jnp.einsum('tk,tkd->td', a, vg)

    return lax.map(block, xt).reshape(B, S, D)


def setup_inputs(seed: int = 0) -> dict:
    key = jax.random.key(seed)
    ks = jax.random.split(key, 20)
    f32 = jnp.float32
    nrm = lambda k, shape, scale: jax.random.normal(k, shape, f32) * scale
    return {
        "x": nrm(ks[0], (BATCH, SEQ, D_MODEL), 1.0),
        "norm_mix": 1.0 + nrm(ks[1], (DEPTH, D_MODEL), 0.02),
        "w_in": nrm(ks[2], (DEPTH, D_MODEL, IN_WIDTH), D_MODEL ** -0.5),
        "hg_lb": nrm(ks[3], (DEPTH + 1, HG_W), 0.1),
        "hg_norm": 1.0 + nrm(ks[4], (DEPTH, HG_HEADS, HG_DV), 0.02),
        "idx_k_norm_g": 1.0 + nrm(ks[5], (DEPTH, IDX_DH), 0.02),
        "idx_k_norm_b": nrm(ks[6], (DEPTH, IDX_DH), 0.01),
        "rel_bias": nrm(ks[7], (REL_BUCKETS, ATT_HEADS), 0.1),
        "w_up_a": nrm(ks[8], (DEPTH, HG_VW, D_MODEL), HG_VW ** -0.5),
        "w_up_b": nrm(ks[9], (DEPTH, ATT_W, D_MODEL), ATT_W ** -0.5),
        "w_out": nrm(ks[10], (DEPTH, D_MODEL, D_MODEL), D_MODEL ** -0.5),
        "norm_ffn": 1.0 + nrm(ks[11], (DEPTH, D_MODEL), 0.02),
        "peer_wq": nrm(ks[12], (DEPTH, D_MODEL, PEER_HEADS * PEER_QDIM), D_MODEL ** -0.5),
        "peer_keys": nrm(ks[13], (DEPTH, 2, PEER_HEADS, PEER_NKEYS, PEER_HALF), PEER_HALF ** -0.5),
        "peer_u": nrm(ks[14], (DEPTH, PEER_EXPERTS, D_MODEL), D_MODEL ** -0.5),
        "peer_v": nrm(ks[15], (DEPTH, PEER_EXPERTS, D_MODEL), PEER_HEADS ** -0.5),
        "norm_final": 1.0 + nrm(ks[16], (D_MODEL,), 0.02),
    }


def reference(x, norm_mix, w_in, hg_lb, hg_norm, idx_k_norm_g, idx_k_norm_b, rel_bias,
              w_up_a, w_up_b, w_out, norm_ffn, peer_wq, peer_keys, peer_u, peer_v, norm_final):
    lb_all = jnp.cumsum(jax.nn.softmax(hg_lb.astype(jnp.float32), axis=0), axis=0)
    for l in range(DEPTH):
        xn = rmsnorm(x, norm_mix[l])
        proj = xn @ w_in[l]
        (hq, hf, hi, hog, aq, ak, av, iq, ik, iw, ga, gb) = split_cols(proj, IN_SIZES)
        ya = hgrn2_mixer(hq, hf, hi, hog, lb_all[l], hg_norm[l])
        yb = dsa_mixer(aq, ak, av, iq, ik, iw, idx_k_norm_g[l], idx_k_norm_b[l], rel_bias)
        h = jax.nn.sigmoid(ga) * (ya @ w_up_a[l]) + jax.nn.sigmoid(gb) * (yb @ w_up_b[l])
        x = x + h @ w_out[l]
        x = x + peer_ffn(rmsnorm(x, norm_ffn[l]), peer_wq[l], peer_keys[l], peer_u[l], peer_v[l])
    return rmsnorm(x, norm_final)
```

```python
import functools
import math

import jax
import jax.numpy as jnp
import numpy as np
from jax import lax
from jax.experimental import pallas as pl
from jax.experimental.pallas import tpu as pltpu

f32 = jnp.float32
bf16 = jnp.bfloat16
i32 = jnp.int32

D_MODEL = 1024
EPS = 1e-6
HG_HEADS, HG_D = 4, 128
HG_W = HG_HEADS * HG_D
ATT_HEADS, ATT_DH = 8, 64
ATT_W = ATT_HEADS * ATT_DH
IDX_HEADS, IDX_DH = 4, 64
TOPK_MAX = 256
REL_BUCKETS, REL_MAX_DIST = 32, 128
PEER_HEADS, PEER_NKEYS, PEER_HALF, PEER_TOPK = 8, 128, 128, 16
PEER_EXPERTS = PEER_NKEYS * PEER_NKEYS
IN_SIZES = (HG_W, HG_W, HG_W, HG_W, ATT_W, ATT_DH, ATT_DH, IDX_HEADS * IDX_DH, IDX_DH, IDX_HEADS, D_MODEL, D_MODEL)

LANES = 128
VMEM_LIMIT = 56 * 1024 * 1024

COL_BLK = 512
PROJ_W = 10 * COL_BLK
CB_HQ, CB_HF, CB_HI, CB_HOG, CB_AQ, CB_IDX = 0, 1, 2, 3, 4, 5
CB_GA, CB_GB = 3, 4
NEG = -1e30


def _cparams(sem):
    return pltpu.CompilerParams(dimension_semantics=sem, vmem_limit_bytes=VMEM_LIMIT)


def _inproj_kernel(x_ref, g_ref, w_ref, o_ref, xn_ref):
    @pl.when(pl.program_id(1) == 0)
    def _():
        x = x_ref[...]
        ms = jnp.mean(x * x, axis=-1, keepdims=True)
        xn_ref[...] = (x * lax.rsqrt(ms + EPS) * g_ref[...]).astype(bf16)

    o_ref[...] = jnp.dot(xn_ref[...], w_ref[...], preferred_element_type=f32)


def _inproj(x2, g, w, tm=1024):
    n = x2.shape[0]
    return pl.pallas_call(
        _inproj_kernel,
        out_shape=jax.ShapeDtypeStruct((n, PROJ_W), f32),
        grid=(n // tm, PROJ_W // COL_BLK),
        in_specs=[
            pl.BlockSpec((tm, D_MODEL), lambda i, j: (i, 0)),
            pl.BlockSpec((1, D_MODEL), lambda i, j: (0, 0)),
            pl.BlockSpec((D_MODEL, COL_BLK), lambda i, j: (0, j)),
        ],
        out_specs=pl.BlockSpec((tm, COL_BLK), lambda i, j: (i, j)),
        scratch_shapes=[pltpu.VMEM((tm, D_MODEL), bf16)],
        compiler_params=_cparams(("parallel", "arbitrary")),
        name="inproj",
    )(x2, g, w)


def _dsa_prep_kernel(p_ref, g_ref, b_ref, iq_ref, ak_ref, av_ref, ik_ref, w_ref):
    p = p_ref[...]
    iq_ref[...] = p[:, 0:256].astype(bf16)
    ak_ref[...] = p[:, 256:320].astype(bf16)
    av_ref[...] = p[:, 320:384].astype(bf16)
    ik = p[:, 384:448]
    mu = jnp.mean(ik, axis=-1, keepdims=True)
    var = jnp.mean(jnp.square(ik - mu), axis=-1, keepdims=True)
    y = (ik - mu) * lax.rsqrt(var + EPS) * g_ref[...] + b_ref[...]
    ik_ref[...] = y.astype(bf16)
    w_ref[...] = p[:, 448:512] * (IDX_HEADS ** -0.5 * IDX_DH ** -0.5)


def _dsa_prep(proj, g, b, tm=1024):
    n = proj.shape[0]
    row = lambda w: pl.BlockSpec((tm, w), lambda i: (i, 0))
    return pl.pallas_call(
        _dsa_prep_kernel,
        out_shape=(
            jax.ShapeDtypeStruct((n, 256), bf16),
            jax.ShapeDtypeStruct((n, 64), bf16),
            jax.ShapeDtypeStruct((n, 64), bf16),
            jax.ShapeDtypeStruct((n, 64), bf16),
            jax.ShapeDtypeStruct((n, 64), f32),
        ),
        grid=(n // tm,),
        in_specs=[
            pl.BlockSpec((tm, COL_BLK), lambda i: (i, CB_IDX)),
            pl.BlockSpec((1, 64), lambda i: (0, 0)),
            pl.BlockSpec((1, 64), lambda i: (0, 0)),
        ],
        out_specs=(row(256), row(64), row(64), row(64), row(64)),
        compiler_params=_cparams(("parallel",)),
        name="dsa_prep",
    )(proj, g, b)


HG_CHUNK = 32


def _split3(a):
    a1 = a.astype(bf16)
    r1 = a - a1.astype(f32)
    a2 = r1.astype(bf16)
    r2 = r1 - a2.astype(f32)
    return a1, a2, r2.astype(bf16)


def _hgrn2_kernel(q_ref, f_ref, i_ref, og_ref, lb_ref, gn_ref, o_ref, st_ref, oacc_ref, *, ts):
    c = HG_CHUNK

    @pl.when(pl.program_id(1) == 0)
    def _():
        st_ref[...] = jnp.zeros_like(st_ref)

    r_io = lax.broadcasted_iota(i32, (c, c), 0)
    c_io = lax.broadcasted_iota(i32, (c, c), 1)
    causal = c_io <= r_io
    tri = causal.astype(bf16)
    mid = c // 2 - 1

    for h in range(HG_HEADS):
        sl = slice(HG_D * h, HG_D * (h + 1))
        lb = lb_ref[:, sl]

        def chunk(ci, carry, sl=sl, lb=lb, h=h):
            rows = pl.ds(pl.multiple_of(ci * c, c), c)
            qr = q_ref[rows, sl]
            q = qr * jax.nn.sigmoid(qr)
            fg = lb + (1.0 - lb) * jax.nn.sigmoid(f_ref[rows, sl])
            k = 1.0 - fg
            v = i_ref[rows, sl].astype(bf16)
            l1, l2, l3 = _split3(jnp.log(fg))
            b = (jnp.dot(tri, l1, preferred_element_type=f32)
                 + jnp.dot(tri, l2, preferred_element_type=f32)
                 + jnp.dot(tri, l3, preferred_element_type=f32))
            b_mid = b[mid:mid + 1, :]
            b_last = b[c - 1:c, :]
            qt = (q * jnp.exp(b - b_mid)).astype(bf16)
            kt = (k * jnp.exp(b_mid - b)).astype(bf16)
            a = lax.dot_general(qt, kt, (((1,), (1,)), ((), ())), preferred_element_type=f32)
            a = jnp.where(causal, a, 0.0).astype(bf16)
            intra = jnp.dot(a, v, preferred_element_type=f32)
            st = st_ref[h]
            qe = (q * jnp.exp(b)).astype(bf16)
            inter = lax.dot_general(qe, st.astype(bf16), (((1,), (1,)), ((), ())), preferred_element_type=f32)
            oacc_ref[rows, sl] = inter + intra
            ke = (k * jnp.exp(b_last - b)).astype(bf16)
            upd = lax.dot_general(v, ke, (((0,), (0,)), ((), ())), preferred_element_type=f32)
            st_ref[h] = st * jnp.exp(b_last) + upd
            return carry

        lax.fori_loop(0, ts // c, chunk, 0)

    for h in range(HG_HEADS):
        sl = slice(HG_D * h, HG_D * (h + 1))
        o = oacc_ref[:, sl]
        ms = jnp.mean(o * o, axis=-1, keepdims=True)
        og = og_ref[:, sl]
        y = o * lax.rsqrt(ms + EPS) * gn_ref[:, sl] * (og * jax.nn.sigmoid(og))
        o_ref[:, sl] = y.astype(o_ref.dtype)


def _hgrn2(proj, lb, gn, batch, seq, ts=256):
    n = proj.shape[0]
    nsb = seq // ts
    col = lambda cb: pl.BlockSpec((ts, COL_BLK), lambda b, s, cb=cb: (b * nsb + s, cb))
    return pl.pallas_call(
        functools.partial(_hgrn2_kernel, ts=ts),
        out_shape=jax.ShapeDtypeStruct((n, HG_W), bf16),
        grid=(batch, nsb),
        in_specs=[col(CB_HQ), col(CB_HF), col(CB_HI), col(CB_HOG),
                  pl.BlockSpec((1, HG_W), lambda b, s: (0, 0)),
                  pl.BlockSpec((1, HG_W), lambda b, s: (0, 0))],
        out_specs=pl.BlockSpec((ts, HG_W), lambda b, s: (b * nsb + s, 0)),
        scratch_shapes=[pltpu.VMEM((HG_HEADS, HG_D, HG_D), f32), pltpu.VMEM((ts, HG_W), f32)],
        compiler_params=_cparams(("parallel", "arbitrary")),
        name="hgrn2",
    )(proj, proj, proj, proj, lb, gn)


DSA_T = 128


def _key_to_float(key):
    bits = jnp.where(key >= 0, key, key ^ jnp.int32(0x7FFFFFFF))
    return pltpu.bitcast(bits, f32)


def _dsa_kernel(iq_ref, w_ref, aq_ref, ik_ref, ak_ref, av_ref, bias_ref, o_ref, sc_ref, qs_ref, *, ktop):
    t = DSA_T
    qi = pl.program_id(1)
    nkb = qi + 1
    row = lax.broadcasted_iota(i32, (t, t), 0)
    colv = lax.broadcasted_iota(i32, (t, t), 1)

    iq = iq_ref[...]
    w = w_ref[...]
    iqs = [iq[:, IDX_DH * j:IDX_DH * (j + 1)] for j in range(IDX_HEADS)]
    ws = [w[:, j:j + 1] for j in range(IDX_HEADS)]

    def score_blk(kb, carry):
        kblk = ik_ref[pl.ds(pl.multiple_of(kb * t, t), t), :]
        s = jnp.zeros((t, t), f32)
        for j in range(IDX_HEADS):
            rel = lax.dot_general(iqs[j], kblk, (((1,), (1,)), ((), ())), preferred_element_type=f32)
            s = s + ws[j] * jnp.maximum(rel, 0.0)
        s = jnp.where((kb < qi) | (colv <= row), s, -jnp.inf)
        sc_ref[kb] = s
        return carry

    lax.fori_loop(0, nkb, score_blk, 0)

    def count_rows(pred):
        def body(kb, acc):
            return acc + pred(sc_ref[kb], kb).astype(i32)
        acc = lax.fori_loop(0, nkb, body, jnp.zeros((t, t), i32))
        return jnp.sum(acc, axis=-1, keepdims=True)

    def bit_step(i, key):
        cand = key + jnp.left_shift(jnp.int32(1), 31 - i)
        cf = _key_to_float(cand)
        cnt = count_rows(lambda s, kb: s >= cf)
        return jnp.where(cnt >= ktop, cand, key)

    key = lax.fori_loop(0, 32, bit_step, jnp.full((t, 1), -2 ** 31, i32))
    thr = _key_to_float(key)
    qpos = qi * t + lax.broadcasted_iota(i32, (t, 1), 0)
    take_all = qpos < ktop
    thr = jnp.where(take_all, -jnp.inf, thr)

    cnt_gt = count_rows(lambda s, kb: s > thr)
    cnt_eq = count_rows(lambda s, kb: s == thr)
    need = ktop - cnt_gt
    excess = jnp.max(jnp.where(take_all, 0, cnt_eq - need)) > 0

    def cut_search(_):
        def cstep(i, cut):
            cand = cut + jnp.left_shift(jnp.int32(1), 12 - i)
            cnt = count_rows(lambda s, kb: (s == thr) & ((colv + kb * t) < cand))
            return jnp.where(cnt < need, cand, cut)
        return lax.fori_loop(0, 13, cstep, jnp.zeros((t, 1), i32))

    cut = lax.cond(excess, cut_search, lambda _: jnp.full((t, 1), 2 ** 30, i32), 0)

    aq = aq_ref[...] * (ATT_DH ** -0.5)
    for h in range(ATT_HEADS):
        qs_ref[t * h:t * (h + 1), :] = aq[:, ATT_DH * h:ATT_DH * (h + 1)].astype(bf16)
    qs = qs_ref[...]

    def attn_blk(kb, carry):
        m, l, acc = carry
        ksl = pl.ds(pl.multiple_of(kb * t, t), t)
        s = sc_ref[kb]
        sel = (s > thr) | ((s == thr) & ((colv + kb * t) <= cut))
        sel = sel & ((kb < qi) | (colv <= row))
        lg = lax.dot_general(qs, ak_ref[ksl, :], (((1,), (1,)), ((), ())), preferred_element_type=f32)
        lg = lg.reshape(ATT_HEADS, t, t) + bias_ref[jnp.minimum(qi - kb, 2)]
        lg = jnp.where(sel[None], lg, NEG)
        m_new = jnp.maximum(m, jnp.max(lg, axis=-1, keepdims=True))
        alpha = jnp.exp(m - m_new)
        p = jnp.exp(lg - m_new)
        l = alpha * l + jnp.sum(p, axis=-1, keepdims=True)
        pv = jnp.dot(p.reshape(ATT_HEADS * t, t).astype(bf16), av_ref[ksl, :], preferred_element_type=f32)
        acc = alpha * acc + pv.reshape(ATT_HEADS, t, ATT_DH)
        return m_new, l, acc

    m0 = jnp.full((ATT_HEADS, t, 1), NEG, f32)
    l0 = jnp.zeros((ATT_HEADS, t, 1), f32)
    a0 = jnp.zeros((ATT_HEADS, t, ATT_DH), f32)
    m, l, acc = lax.fori_loop(0, nkb, attn_blk, (m0, l0, a0))
    out = acc / l
    o_ref[...] = jnp.concatenate([out[h] for h in range(ATT_HEADS)], axis=-1).astype(o_ref.dtype)


def _rel_bias_tiles(rel_bias):
    t = DSA_T
    i = jnp.arange(t)[:, None]
    j = jnp.arange(t)[None, :]
    tiles = []
    for v in range(3):
        dist = jnp.maximum(v * t + i - j, 0)
        max_exact = REL_BUCKETS // 2
        nf = jnp.maximum(dist, 1).astype(f32)
        large = max_exact + (jnp.log(nf / max_exact) / math.log(REL_MAX_DIST / max_exact)
                             * (REL_BUCKETS - max_exact)).astype(i32)
        large = jnp.minimum(large, REL_BUCKETS - 1)
        bucket = jnp.where(dist < max_exact, dist, large)
        tiles.append(jnp.transpose(rel_bias[bucket].astype(f32), (2, 0, 1)))
    return jnp.stack(tiles)


def _dsa(proj, iq, w, ik, ak, av, bias_tiles, batch, seq):
    n = proj.shape[0]
    t = DSA_T
    nq = seq // t
    ktop = min(TOPK_MAX, seq // 4)
    qrow = lambda width: pl.BlockSpec((t, width), lambda b, q: (b * nq + q, 0))
    kv = pl.BlockSpec((seq, 64), lambda b, q: (b, 0))
    return pl.pallas_call(
        functools.partial(_dsa_kernel, ktop=ktop),
        out_shape=jax.ShapeDtypeStruct((n, ATT_W), bf16),
        grid=(batch, nq),
        in_specs=[qrow(256), qrow(64),
                  pl.BlockSpec((t, COL_BLK), lambda b, q: (b * nq + q, CB_AQ)),
                  kv, kv, kv,
                  pl.BlockSpec((3, ATT_HEADS, t, t), lambda b, q: (0, 0, 0, 0))],
        out_specs=qrow(ATT_W),
        scratch_shapes=[pltpu.VMEM((nq, t, t), f32), pltpu.VMEM((ATT_HEADS * t, ATT_DH), bf16)],
        compiler_params=_cparams(("parallel", "arbitrary")),
        name="dsa",
    )(iq, w, proj, ik, ak, av, bias_tiles)


def _merge_kernel(x_ref, ya_ref, yb_ref, ga_ref, gb_ref, wa_ref, wb_ref, wo_ref, g2_ref, wq_ref,
                  x1_ref, xn_ref, qp_ref):
    ha = jnp.dot(ya_ref[...], wa_ref[...], preferred_element_type=f32)
    hb = jnp.dot(yb_ref[...], wb_ref[...], preferred_element_type=f32)
    h = jax.nn.sigmoid(ga_ref[...]) * ha + jax.nn.sigmoid(gb_ref[...]) * hb
    x1 = x_ref[...] + jnp.dot(h.astype(bf16), wo_ref[...], preferred_element_type=f32)
    x1_ref[...] = x1
    ms = jnp.mean(x1 * x1, axis=-1, keepdims=True)
    xn = (x1 * lax.rsqrt(ms + EPS) * g2_ref[...]).astype(bf16)
    xn_ref[...] = xn
    qp_ref[...] = jnp.dot(xn, wq_ref[...], preferred_element_type=f32).astype(bf16)


def _merge(x2, ya, yb, proj, wa, wb, wo, g2, wq, tm=512):
    n = x2.shape[0]
    qw = wq.shape[1]
    row = lambda width: pl.BlockSpec((tm, width), lambda i: (i, 0))
    full = lambda a: pl.BlockSpec(a.shape, lambda i: (0, 0))
    return pl.pallas_call(
        _merge_kernel,
        out_shape=(jax.ShapeDtypeStruct((n, D_MODEL), f32),
                   jax.ShapeDtypeStruct((n, D_MODEL), bf16),
                   jax.ShapeDtypeStruct((n, qw), bf16)),
        grid=(n // tm,),
        in_specs=[row(D_MODEL), row(HG_W), row(ATT_W),
                  pl.BlockSpec((tm, D_MODEL), lambda i: (i, CB_GA)),
                  pl.BlockSpec((tm, D_MODEL), lambda i: (i, CB_GB)),
                  full(wa), full(wb), full(wo), full(g2), full(wq)],
        out_specs=(row(D_MODEL), row(D_MODEL), row(qw)),
        compiler_params=_cparams(("parallel",)),
        name="merge",
    )(x2, ya, yb, proj, proj, wa, wb, wo, g2, wq)


_PAIR_COUNTS = tuple(PEER_TOPK // (p + 1) for p in range(PEER_TOPK))
_NCAND = 56


def _top_rows(s, nrows):
    out = []
    rowi = lax.broadcasted_iota(i32, s.shape, 0)
    for _ in range(nrows):
        m = jnp.max(s, axis=0, keepdims=True)
        out.append(m)
        first = jnp.min(jnp.where(s == m, rowi, s.shape[0]), axis=0, keepdims=True)
        s = jnp.where(rowi == first, -jnp.inf, s)
    return out


def _peer_route_kernel(qp_ref, k1_ref, k2_ref, s1_ref, s2_ref, a1_ref, e2_ref, thr_ref, cand_ref):
    tb = qp_ref.shape[0]
    for h in range(PEER_HEADS):
        q1 = qp_ref[:, 2 * PEER_HALF * h:2 * PEER_HALF * h + PEER_HALF]
        q2 = qp_ref[:, 2 * PEER_HALF * h + PEER_HALF:2 * PEER_HALF * (h + 1)]
        nt = (((1,), (1,)), ((), ()))
        s1 = lax.dot_general(k1_ref[h], q1, nt, preferred_element_type=f32)
        s2 = lax.dot_general(k2_ref[h], q2, nt, preferred_element_type=f32)
        a = _top_rows(s1, PEER_TOPK)
        b = _top_rows(s2, PEER_TOPK)
        cand_ref[...] = jnp.full((_NCAND, tb), -jnp.inf, f32)
        r = 0
        for p in range(PEER_TOPK):
            for q in range(_PAIR_COUNTS[p]):
                cand_ref[r:r + 1, :] = a[p] + b[q]
                r += 1
        top = _top_rows(cand_ref[...], PEER_TOPK)
        mx = a[0] + b[0]
        z = jnp.zeros_like(mx)
        for tv in top:
            z = z + jnp.exp(tv - mx)
        s1_ref[h] = s1
        s2_ref[h] = s2
        a1_ref[h] = jnp.exp(s1 - a[0]) / z
        e2_ref[h] = jnp.exp(s2 - b[0])
        thr_ref[h:h + 1, :] = top[PEER_TOPK - 1]


def _peer_route(qp, k1, k2, tb=512):
    n = qp.shape[0]
    big = lambda: pl.BlockSpec((PEER_HEADS, PEER_NKEYS, tb), lambda i: (0, 0, i))
    bigs = jax.ShapeDtypeStruct((PEER_HEADS, PEER_NKEYS, n), f32)
    return pl.pallas_call(
        _peer_route_kernel,
        out_shape=(bigs, bigs, bigs, bigs, jax.ShapeDtypeStruct((PEER_HEADS, n), f32)),
        grid=(n // tb,),
        in_specs=[pl.BlockSpec((tb, qp.shape[1]), lambda i: (i, 0)),
                  pl.BlockSpec(k1.shape, lambda i: (0, 0, 0)),
                  pl.BlockSpec(k2.shape, lambda i: (0, 0, 0))],
        out_specs=(big(), big(), big(), big(), pl.BlockSpec((PEER_HEADS, tb), lambda i: (0, i))),
        scratch_shapes=[pltpu.VMEM((_NCAND, tb), f32)],
        compiler_params=_cparams(("parallel",)),
        name="peer_route",
    )(qp, k1, k2)


PEER_EB = 1024


def _peer_dense_kernel(xn_ref, u_ref, vt_ref, s1_ref, s2_ref, a1_ref, e2_ref, thr_ref, x1_ref, gf_ref,
                       o_ref, acc_ref, act_ref, *, final_norm):
    e = pl.program_id(1)
    tb = xn_ref.shape[0]

    @pl.when(e == 0)
    def _():
        acc_ref[...] = jnp.zeros_like(acc_ref)

    ht = lax.dot_general(u_ref[...], xn_ref[...], (((1,), (1,)), ((), ())), preferred_element_type=f32)
    for ii in range(PEER_EB // PEER_NKEYS):
        i = e * (PEER_EB // PEER_NKEYS) + ii
        g = jnp.zeros((PEER_NKEYS, tb), f32)
        for h in range(PEER_HEADS):
            c = s1_ref[h, pl.ds(i, 1), :]
            a = a1_ref[h, pl.ds(i, 1), :]
            hit = (s2_ref[h] + c) >= thr_ref[h:h + 1, :]
            g = g + jnp.where(hit, e2_ref[h] * a, 0.0)
        hh = ht[PEER_NKEYS * ii:PEER_NKEYS * (ii + 1), :]
        act = 0.5 * hh * (1.0 + lax.erf(hh * (2.0 ** -0.5)))
        act_ref[PEER_NKEYS * ii:PEER_NKEYS * (ii + 1), :] = (act * g).astype(bf16)
    acc_ref[...] += jnp.dot(vt_ref[...], act_ref[...], preferred_element_type=f32)

    @pl.when(e == pl.num_programs(1) - 1)
    def _():
        y = x1_ref[...] + acc_ref[...].T
        if final_norm:
            ms = jnp.mean(y * y, axis=-1, keepdims=True)
            y = y * lax.rsqrt(ms + EPS) * gf_ref[...]
        o_ref[...] = y


def _peer_dense(xn, u, vt, s1, s2, a1, e2, thr, x1, gf, final_norm, tb=256):
    n = xn.shape[0]
    ne = u.shape[0]
    big = lambda: pl.BlockSpec((PEER_HEADS, PEER_NKEYS, tb), lambda t, e: (0, 0, t))
    return pl.pallas_call(
        functools.partial(_peer_dense_kernel, final_norm=final_norm),
        out_shape=jax.ShapeDtypeStruct((n, D_MODEL), f32),
        grid=(n // tb, ne // PEER_EB),
        in_specs=[pl.BlockSpec((tb, D_MODEL), lambda t, e: (t, 0)),
                  pl.BlockSpec((PEER_EB, D_MODEL), lambda t, e: (e, 0)),
                  pl.BlockSpec((D_MODEL, PEER_EB), lambda t, e: (0, e)),
                  big(), big(), big(), big(),
                  pl.BlockSpec((PEER_HEADS, tb), lambda t, e: (0, t)),
                  pl.BlockSpec((tb, D_MODEL), lambda t, e: (t, 0)),
                  pl.BlockSpec((1, D_MODEL), lambda t, e: (0, 0))],
        out_specs=pl.BlockSpec((tb, D_MODEL), lambda t, e: (t, 0)),
        scratch_shapes=[pltpu.VMEM((D_MODEL, tb), f32), pltpu.VMEM((PEER_EB, tb), bf16)],
        compiler_params=_cparams(("parallel", "arbitrary")),
        name="peer_dense",
    )(xn, u, vt, s1, s2, a1, e2, thr, x1, gf)


def _pack_w_in(w):
    offs = np.concatenate([[0], np.cumsum(np.array(IN_SIZES))]).tolist()
    seg = lambda k: w[:, offs[k]:offs[k + 1]]
    hq, hf, hi, hog, aq, ak, av, iq, ik, iw, ga, gb = [seg(k) for k in range(12)]
    pad = jnp.zeros((w.shape[0], 64 - IDX_HEADS), w.dtype)
    return jnp.concatenate([hq, hf, hi, hog, aq, iq, ak, av, ik, iw, pad, ga, gb], axis=1).astype(bf16)


def _layer(x2, batch, seq, norm_mix, w_in, lb, hg_norm, ikg, ikb, bias_tiles, w_up_a, w_up_b, w_out,
           norm_ffn, peer_wq, peer_keys, peer_u, peer_v):
    proj = _inproj(x2, norm_mix.reshape(1, -1), _pack_w_in(w_in))
    iq, ak, av, ik, w = _dsa_prep(proj, ikg.reshape(1, -1), ikb.reshape(1, -1))
    ya = _hgrn2(proj, lb.reshape(1, -1), hg_norm.reshape(1, -1), batch, seq)
    yb = _dsa(proj, iq, w, ik, ak, av, bias_tiles, batch, seq)
    x1, xn, qp = _merge(x2, ya, yb, proj, w_up_a.astype(bf16), w_up_b.astype(bf16), w_out.astype(bf16),
                        norm_ffn.reshape(1, -1), peer_wq.astype(bf16))
    s1, s2, a1, e2, thr = _peer_route(qp, peer_keys[0].astype(bf16), peer_keys[1].astype(bf16))
    return xn, s1, s2, a1, e2, thr, x1


def kernel(x, norm_mix, w_in, hg_lb, hg_norm, idx_k_norm_g, idx_k_norm_b, rel_bias, w_up_a, w_up_b, w_out,
           norm_ffn, peer_wq, peer_keys, peer_u, peer_v, norm_final):
    batch, seq, _ = x.shape
    depth = w_in.shape[0]
    lb_all = jnp.cumsum(jax.nn.softmax(hg_lb.astype(f32), axis=0), axis=0)
    bias_tiles = _rel_bias_tiles(rel_bias)
    x2 = x.reshape(batch * seq, D_MODEL)
    gf = norm_final.reshape(1, -1)
    for l in range(depth):
        xn, s1, s2, a1, e2, thr, x1 = _layer(
            x2, batch, seq, norm_mix[l], w_in[l], lb_all[l], hg_norm[l], idx_k_norm_g[l], idx_k_norm_b[l],
            bias_tiles, w_up_a[l], w_up_b[l], w_out[l], norm_ffn[l], peer_wq[l], peer_keys[l], peer_u[l], peer_v[l])
        x2 = _peer_dense(xn, peer_u[l].astype(bf16), peer_v[l].astype(bf16).T, s1, s2, a1, e2, thr, x1, gf,
                         final_norm=(l == depth - 1))
    return x2.reshape(batch, seq, D_MODEL)
```

```python
import functools
import math

import jax
import jax.numpy as jnp
import numpy as np
from jax import lax
from jax.experimental import pallas as pl
from jax.experimental.pallas import tpu as pltpu

f32 = jnp.float32
bf16 = jnp.bfloat16
i32 = jnp.int32

D_MODEL = 1024
EPS = 1e-6
HG_HEADS, HG_D = 4, 128
HG_W = HG_HEADS * HG_D
ATT_HEADS, ATT_DH = 8, 64
ATT_W = ATT_HEADS * ATT_DH
IDX_HEADS, IDX_DH = 4, 64
TOPK_MAX = 256
REL_BUCKETS, REL_MAX_DIST = 32, 128
PEER_HEADS, PEER_NKEYS, PEER_HALF, PEER_TOPK = 8, 128, 128, 16
PEER_EXPERTS = PEER_NKEYS * PEER_NKEYS
IN_SIZES = (HG_W, HG_W, HG_W, HG_W, ATT_W, ATT_DH, ATT_DH, IDX_HEADS * IDX_DH, IDX_DH, IDX_HEADS, D_MODEL, D_MODEL)

LANES = 128
VMEM_LIMIT = 56 * 1024 * 1024

COL_BLK = 512
PROJ_W = 10 * COL_BLK
CB_HQ, CB_HF, CB_HI, CB_HOG, CB_AQ, CB_IDX = 0, 1, 2, 3, 4, 5
CB_GA, CB_GB = 3, 4
NEG = -1e30


def _cparams(sem):
    return pltpu.CompilerParams(dimension_semantics=sem, vmem_limit_bytes=VMEM_LIMIT)


def _inproj_kernel(x_ref, g_ref, w_ref, o_ref, xn_ref):
    @pl.when(pl.program_id(1) == 0)
    def _():
        x = x_ref[...]
        ms = jnp.mean(x * x, axis=-1, keepdims=True)
        xn_ref[...] = (x * lax.rsqrt(ms + EPS) * g_ref[...]).astype(bf16)

    o_ref[...] = jnp.dot(xn_ref[...], w_ref[...], preferred_element_type=f32)


def _inproj(x2, g, w, tm=1024):
    n = x2.shape[0]
    return pl.pallas_call(
        _inproj_kernel,
        out_shape=jax.ShapeDtypeStruct((n, PROJ_W), f32),
        grid=(n // tm, PROJ_W // COL_BLK),
        in_specs=[
            pl.BlockSpec((tm, D_MODEL), lambda i, j: (i, 0)),
            pl.BlockSpec((1, D_MODEL), lambda i, j: (0, 0)),
            pl.BlockSpec((D_MODEL, COL_BLK), lambda i, j: (0, j)),
        ],
        out_specs=pl.BlockSpec((tm, COL_BLK), lambda i, j: (i, j)),
        scratch_shapes=[pltpu.VMEM((tm, D_MODEL), bf16)],
        compiler_params=_cparams(("parallel", "arbitrary")),
        name="inproj",
    )(x2, g, w)


DSA_T = 128


def _dsa_prep_kernel(p_ref, g_ref, b_ref, iq_ref, ak_ref, ik_ref, avt_ref, wt_ref):
    t = DSA_T
    p = p_ref[...]
    iq_ref[...] = p[:, 0:256].astype(bf16)
    ak_ref[...] = p[:, 256:320].astype(bf16)
    ik = p[:, 384:448]
    mu = jnp.mean(ik, axis=-1, keepdims=True)
    var = jnp.mean(jnp.square(ik - mu), axis=-1, keepdims=True)
    y = (ik - mu) * lax.rsqrt(var + EPS) * g_ref[...] + b_ref[...]
    ik_ref[...] = y.astype(bf16)
    kv_t = p[:, 256:384].T
    for c in range(p.shape[0] // t):
        avt_ref[c] = kv_t[64:128, t * c:t * (c + 1)].astype(bf16)
    wt_ref[...] = p[:, 384:512].T[64:72, :] * (IDX_HEADS ** -0.5 * IDX_DH ** -0.5)


def _dsa_prep(proj, g, b, tm=1024):
    n = proj.shape[0]
    t = DSA_T
    row = lambda w: pl.BlockSpec((tm, w), lambda i: (i, 0))
    return pl.pallas_call(
        _dsa_prep_kernel,
        out_shape=(
            jax.ShapeDtypeStruct((n, 256), bf16),
            jax.ShapeDtypeStruct((n, 64), bf16),
            jax.ShapeDtypeStruct((n, 64), bf16),
            jax.ShapeDtypeStruct((n // t, 64, t), bf16),
            jax.ShapeDtypeStruct((8, n), f32),
        ),
        grid=(n // tm,),
        in_specs=[
            pl.BlockSpec((tm, COL_BLK), lambda i: (i, CB_IDX)),
            pl.BlockSpec((1, 64), lambda i: (0, 0)),
            pl.BlockSpec((1, 64), lambda i: (0, 0)),
        ],
        out_specs=(row(256), row(64), row(64),
                   pl.BlockSpec((tm // t, 64, t), lambda i: (i, 0, 0)),
                   pl.BlockSpec((8, tm), lambda i: (0, i))),
        compiler_params=_cparams(("parallel",)),
        name="dsa_prep",
    )(proj, g, b)


HG_CHUNK = 32


def _split3(a):
    a1 = a.astype(bf16)
    r1 = a - a1.astype(f32)
    a2 = r1.astype(bf16)
    r2 = r1 - a2.astype(f32)
    return a1, a2, r2.astype(bf16)


def _hgrn2_kernel(q_ref, f_ref, i_ref, og_ref, lb_ref, gn_ref, o_ref, st_ref, oacc_ref, *, ts):
    c = HG_CHUNK

    @pl.when(pl.program_id(1) == 0)
    def _():
        st_ref[...] = jnp.zeros_like(st_ref)

    r_io = lax.broadcasted_iota(i32, (c, c), 0)
    c_io = lax.broadcasted_iota(i32, (c, c), 1)
    causal = c_io <= r_io
    tri = causal.astype(bf16)
    mid = c // 2 - 1

    for h in range(HG_HEADS):
        sl = slice(HG_D * h, HG_D * (h + 1))
        lb = lb_ref[:, sl]

        def chunk(ci, carry, sl=sl, lb=lb, h=h):
            rows = pl.ds(pl.multiple_of(ci * c, c), c)
            qr = q_ref[rows, sl]
            q = qr * jax.nn.sigmoid(qr)
            fg = lb + (1.0 - lb) * jax.nn.sigmoid(f_ref[rows, sl])
            k = 1.0 - fg
            v = i_ref[rows, sl].astype(bf16)
            l1, l2, l3 = _split3(jnp.log(fg))
            b = (jnp.dot(tri, l1, preferred_element_type=f32)
                 + jnp.dot(tri, l2, preferred_element_type=f32)
                 + jnp.dot(tri, l3, preferred_element_type=f32))
            b_mid = b[mid:mid + 1, :]
            b_last = b[c - 1:c, :]
            qt = (q * jnp.exp(b - b_mid)).astype(bf16)
            kt = (k * jnp.exp(b_mid - b)).astype(bf16)
            a = lax.dot_general(qt, kt, (((1,), (1,)), ((), ())), preferred_element_type=f32)
            a = jnp.where(causal, a, 0.0).astype(bf16)
            intra = jnp.dot(a, v, preferred_element_type=f32)
            st = st_ref[h]
            qe = (q * jnp.exp(b)).astype(bf16)
            inter = lax.dot_general(qe, st.astype(bf16), (((1,), (1,)), ((), ())), preferred_element_type=f32)
            oacc_ref[rows, sl] = inter + intra
            ke = (k * jnp.exp(b_last - b)).astype(bf16)
            upd = lax.dot_general(v, ke, (((0,), (0,)), ((), ())), preferred_element_type=f32)
            st_ref[h] = st * jnp.exp(b_last) + upd
            return carry

        lax.fori_loop(0, ts // c, chunk, 0)

    for h in range(HG_HEADS):
        sl = slice(HG_D * h, HG_D * (h + 1))
        o = oacc_ref[:, sl]
        ms = jnp.mean(o * o, axis=-1, keepdims=True)
        og = og_ref[:, sl]
        y = o * lax.rsqrt(ms + EPS) * gn_ref[:, sl] * (og * jax.nn.sigmoid(og))
        o_ref[:, sl] = y.astype(o_ref.dtype)


def _hgrn2(proj, lb, gn, batch, seq, ts=256):
    n = proj.shape[0]
    nsb = seq // ts
    col = lambda cb: pl.BlockSpec((ts, COL_BLK), lambda b, s, cb=cb: (b * nsb + s, cb))
    return pl.pallas_call(
        functools.partial(_hgrn2_kernel, ts=ts),
        out_shape=jax.ShapeDtypeStruct((n, HG_W), bf16),
        grid=(batch, nsb),
        in_specs=[col(CB_HQ), col(CB_HF), col(CB_HI), col(CB_HOG),
                  pl.BlockSpec((1, HG_W), lambda b, s: (0, 0)),
                  pl.BlockSpec((1, HG_W), lambda b, s: (0, 0))],
        out_specs=pl.BlockSpec((ts, HG_W), lambda b, s: (b * nsb + s, 0)),
        scratch_shapes=[pltpu.VMEM((HG_HEADS, HG_D, HG_D), f32), pltpu.VMEM((ts, HG_W), f32)],
        compiler_params=_cparams(("parallel", "arbitrary")),
        name="hgrn2",
    )(proj, proj, proj, proj, lb, gn)


LOG2E = 1.4426950408889634
FLT_MAX = 3.4028234663852886e38
NT_DIMS = (((1,), (1,)), ((), ()))


def _key_to_float(key):
    bits = jnp.where(key >= 0, key, key ^ jnp.int32(0x7FFFFFFF))
    return pltpu.bitcast(bits, f32)


def _dsa_kernel(iq_ref, wt_ref, aq_ref, ik_ref, ak_ref, avt_ref, bias_ref, o_ref,
                sc_ref, iqs_ref, qs_ref, lg_ref, pt_ref, al_ref, acc_ref, *, ktop):
    t = DSA_T
    qi = pl.program_id(1)
    npair = (qi + 2) // 2
    krow = lax.broadcasted_iota(i32, (t, t), 0)
    qcol = lax.broadcasted_iota(i32, (t, t), 1)
    qpos = qi * t + lax.broadcasted_iota(i32, (1, t), 1)

    iq = iq_ref[...]
    for j in range(IDX_HEADS):
        iqs_ref[t * j:t * (j + 1), :] = iq[:, IDX_DH * j:IDX_DH * (j + 1)]
    aq = aq_ref[...] * (ATT_DH ** -0.5 * LOG2E)
    for h in range(ATT_HEADS):
        qs_ref[t * h:t * (h + 1), :] = aq[:, ATT_DH * h:ATT_DH * (h + 1)].astype(bf16)
    wt = wt_ref[...]

    def score_pair(kp):
        ksl = pl.ds(pl.multiple_of(kp * 2 * t, 2 * t), 2 * t)
        rel = lax.dot_general(ik_ref[ksl, :], iqs_ref[...], NT_DIMS, preferred_element_type=f32)
        for u in range(2):
            kb = 2 * kp + u
            s = jnp.zeros((t, t), f32)
            for j in range(IDX_HEADS):
                s = s + wt[j:j + 1, :] * jnp.maximum(rel[t * u:t * (u + 1), t * j:t * (j + 1)], 0.0)
            sc_ref[kb] = jnp.where((kb * t + krow) <= (qi * t + qcol), s, -jnp.inf)

    def score_quad(kq, carry):
        score_pair(2 * kq)
        score_pair(2 * kq + 1)
        return carry

    lax.fori_loop(0, (qi + 4) // 4, score_quad, 0)

    def count_keys(pred):
        def body(kp, acc):
            acc = acc + pred(sc_ref[2 * kp], 2 * kp).astype(i32)
            return acc + pred(sc_ref[2 * kp + 1], 2 * kp + 1).astype(i32)
        acc = lax.fori_loop(0, npair, body, jnp.zeros((t, t), i32))
        return jnp.sum(acc, axis=0, keepdims=True)

    take_all = qpos < ktop

    def unsettled(cnt):
        return jnp.max(jnp.where(take_all, ktop, cnt)) > ktop

    bits_per_check = 4

    def bit_cond(c):
        return (c[0] < 32 // bits_per_check) & unsettled(c[2])

    def bit_step(c):
        g, key, cnt = c
        for b in range(bits_per_check):
            cand = key + jnp.left_shift(jnp.int32(1), 31 - b - bits_per_check * g)
            cf = _key_to_float(cand)
            cc = count_keys(lambda s, kb: s >= cf)
            ok = cc >= ktop
            key = jnp.where(ok, cand, key)
            cnt = jnp.where(ok, cc, cnt)
        return g + 1, key, cnt

    _, key, cnt = lax.while_loop(bit_cond, bit_step,
                                 (jnp.int32(0), jnp.full((1, t), -2 ** 31, i32), qpos + 1))
    thr = jnp.where(take_all, -FLT_MAX, _key_to_float(key))

    @pl.when(unsettled(cnt))
    def _():
        need = ktop - count_keys(lambda s, kb: s > thr)

        def cstep(i, cut):
            cand = cut + jnp.left_shift(jnp.int32(1), 12 - i)
            c = count_keys(lambda s, kb: (s == thr) & ((krow + kb * t) < cand))
            return jnp.where(c < need, cand, cut)

        cut = lax.fori_loop(0, 13, cstep, jnp.zeros((1, t), i32))

        def drop(kb, carry):
            s = sc_ref[kb]
            sc_ref[kb] = jnp.where((s == thr) & ((krow + kb * t) > cut), -jnp.inf, s)
            return carry

        lax.fori_loop(0, 2 * npair, drop, 0)

    nq = sc_ref.shape[0]
    hw = ATT_HEADS * t
    acc_ref[...] = jnp.zeros(acc_ref.shape, f32)
    pt_ref[1] = jnp.zeros(pt_ref.shape[1:], bf16)
    al_ref[1] = jnp.ones(al_ref.shape[1:], f32)

    def logit_stage(j, par):
        for u in range(2):
            kb = jnp.minimum(2 * j + u, nq - 1)
            kblk = ak_ref[pl.ds(pl.multiple_of(kb * t, t), t), :]
            lg_ref[par, u] = lax.dot_general(kblk, qs_ref[...], NT_DIMS, preferred_element_type=f32)

    def softmax_stage(j, par, m_all, l_all):
        for u in range(2):
            kb = 2 * j + u
            madd = jnp.where(sc_ref[kb] >= thr, 0.0, NEG)
            near = jnp.clip(qi - kb, 0, 2)
            ms, ls, als = [], [], []
            for h in range(ATT_HEADS):
                hs = slice(t * h, t * (h + 1))
                lg = lg_ref[par, u, :, hs] + (madd + bias_ref[near, h])
                m_old = m_all[:, hs]
                m_new = jnp.maximum(m_old, jnp.max(lg, axis=0, keepdims=True))
                alpha = jnp.exp2(m_old - m_new)
                p = jnp.exp2(lg - m_new)
                ls.append(alpha * l_all[:, hs] + jnp.sum(p, axis=0, keepdims=True))
                ms.append(m_new)
                als.append(alpha)
                pt_ref[par, u, :, hs] = p.astype(bf16)
            al_ref[par, u] = jnp.concatenate(als, axis=1)
            m_all, l_all = jnp.concatenate(ms, axis=1), jnp.concatenate(ls, axis=1)
        return m_all, l_all

    def value_stage(j, par):
        for u in range(2):
            kb = jnp.maximum(2 * j + u, 0)
            pv = jnp.dot(avt_ref[kb], pt_ref[par, u], preferred_element_type=f32)
            acc_ref[...] = acc_ref[...] * al_ref[par, u] + pv

    def attn_trip(q, carry):
        for par in range(2):
            j = 2 * q + par
            value_stage(j - 1, 1 - par)
            carry = softmax_stage(j, par, *carry)
            logit_stage(j + 1, 1 - par)
        return carry

    nquad = (qi + 4) // 4
    logit_stage(0, 0)
    _, l_all = lax.fori_loop(0, nquad, attn_trip, (jnp.full((1, hw), NEG, f32), jnp.zeros((1, hw), f32)))
    value_stage(2 * nquad - 1, 1)

    out_t = acc_ref[...] / l_all
    out_t = jnp.concatenate([out_t[:, t * h:t * (h + 1)] for h in range(ATT_HEADS)], axis=0)
    o_ref[...] = out_t.T.astype(o_ref.dtype)


def _rel_bias_tiles(rel_bias):
    t = DSA_T
    ki = jnp.arange(t)[:, None]
    qj = jnp.arange(t)[None, :]
    far = rel_bias[REL_BUCKETS - 1].astype(f32)
    tiles = []
    for v in range(2):
        dist = jnp.maximum(v * t + qj - ki, 0)
        max_exact = REL_BUCKETS // 2
        nf = jnp.maximum(dist, 1).astype(f32)
        large = max_exact + (jnp.log(nf / max_exact) / math.log(REL_MAX_DIST / max_exact)
                             * (REL_BUCKETS - max_exact)).astype(i32)
        large = jnp.minimum(large, REL_BUCKETS - 1)
        bucket = jnp.where(dist < max_exact, dist, large)
        tiles.append(jnp.transpose((rel_bias[bucket].astype(f32) - far) * LOG2E, (2, 0, 1)))
    tiles.append(jnp.zeros_like(tiles[0]))
    return jnp.stack(tiles)


def _dsa(proj, iq, wt, ik, ak, avt, bias_tiles, batch, seq):
    n = proj.shape[0]
    t = DSA_T
    nq = seq // t
    assert nq % 4 == 0 and 2 * t >= REL_MAX_DIST
    ktop = min(TOPK_MAX, seq // 4)
    qrow = lambda width: pl.BlockSpec((t, width), lambda b, q: (b * nq + q, 0))
    kv = pl.BlockSpec((seq, 64), lambda b, q: (b, 0))
    hw = ATT_HEADS * t
    return pl.pallas_call(
        functools.partial(_dsa_kernel, ktop=ktop),
        out_shape=jax.ShapeDtypeStruct((n, ATT_W), bf16),
        grid=(batch, nq),
        in_specs=[qrow(256),
                  pl.BlockSpec((8, t), lambda b, q: (0, b * nq + q)),
                  pl.BlockSpec((t, COL_BLK), lambda b, q: (b * nq + q, CB_AQ)),
                  kv, kv,
                  pl.BlockSpec((nq, 64, t), lambda b, q: (b, 0, 0)),
                  pl.BlockSpec((3, ATT_HEADS, t, t), lambda b, q: (0, 0, 0, 0))],
        out_specs=qrow(ATT_W),
        scratch_shapes=[pltpu.VMEM((nq, t, t), f32),
                        pltpu.VMEM((IDX_HEADS * t, IDX_DH), bf16),
                        pltpu.VMEM((hw, ATT_DH), bf16),
                        pltpu.VMEM((2, 2, t, hw), f32),
                        pltpu.VMEM((2, 2, t, hw), bf16),
                        pltpu.VMEM((2, 2, 1, hw), f32),
                        pltpu.VMEM((ATT_DH, hw), f32)],
        compiler_params=_cparams(("parallel", "arbitrary")),
        name="dsa",
    )(iq, wt, proj, ik, ak, avt, bias_tiles)


def _merge_kernel(x_ref, ya_ref, yb_ref, ga_ref, gb_ref, wa_ref, wb_ref, wo_ref, g2_ref, wq_ref,
                  x1_ref, xn_ref, qp_ref):
    ha = jnp.dot(ya_ref[...], wa_ref[...], preferred_element_type=f32)
    hb = jnp.dot(yb_ref[...], wb_ref[...], preferred_element_type=f32)
    h = jax.nn.sigmoid(ga_ref[...]) * ha + jax.nn.sigmoid(gb_ref[...]) * hb
    x1 = x_ref[...] + jnp.dot(h.astype(bf16), wo_ref[...], preferred_element_type=f32)
    x1_ref[...] = x1
    ms = jnp.mean(x1 * x1, axis=-1, keepdims=True)
    xn = (x1 * lax.rsqrt(ms + EPS) * g2_ref[...]).astype(bf16)
    xn_ref[...] = xn
    qp_ref[...] = jnp.dot(xn, wq_ref[...], preferred_element_type=f32).astype(bf16)


def _merge(x2, ya, yb, proj, wa, wb, wo, g2, wq, tm=512):
    n = x2.shape[0]
    qw = wq.shape[1]
    row = lambda width: pl.BlockSpec((tm, width), lambda i: (i, 0))
    full = lambda a: pl.BlockSpec(a.shape, lambda i: (0, 0))
    return pl.pallas_call(
        _merge_kernel,
        out_shape=(jax.ShapeDtypeStruct((n, D_MODEL), f32),
                   jax.ShapeDtypeStruct((n, D_MODEL), bf16),
                   jax.ShapeDtypeStruct((n, qw), bf16)),
        grid=(n // tm,),
        in_specs=[row(D_MODEL), row(HG_W), row(ATT_W),
                  pl.BlockSpec((tm, D_MODEL), lambda i: (i, CB_GA)),
                  pl.BlockSpec((tm, D_MODEL), lambda i: (i, CB_GB)),
                  full(wa), full(wb), full(wo), full(g2), full(wq)],
        out_specs=(row(D_MODEL), row(D_MODEL), row(qw)),
        compiler_params=_cparams(("parallel",)),
        name="merge",
    )(x2, ya, yb, proj, proj, wa, wb, wo, g2, wq)


_PAIR_COUNTS = tuple(PEER_TOPK // (p + 1) for p in range(PEER_TOPK))
_NCAND = 56


def _top_rows(s, nrows):
    out = []
    rowi = lax.broadcasted_iota(i32, s.shape, 0)
    for _ in range(nrows):
        m = jnp.max(s, axis=0, keepdims=True)
        out.append(m)
        first = jnp.min(jnp.where(s == m, rowi, s.shape[0]), axis=0, keepdims=True)
        s = jnp.where(rowi == first, -jnp.inf, s)
    return out


def _peer_route_kernel(qp_ref, k1_ref, k2_ref, s1_ref, s2_ref, a1_ref, e2_ref, thr_ref, cand_ref):
    tb = qp_ref.shape[0]
    for h in range(PEER_HEADS):
        q1 = qp_ref[:, 2 * PEER_HALF * h:2 * PEER_HALF * h + PEER_HALF]
        q2 = qp_ref[:, 2 * PEER_HALF * h + PEER_HALF:2 * PEER_HALF * (h + 1)]
        nt = (((1,), (1,)), ((), ()))
        s1 = lax.dot_general(k1_ref[h], q1, nt, preferred_element_type=f32)
        s2 = lax.dot_general(k2_ref[h], q2, nt, preferred_element_type=f32)
        a = _top_rows(s1, PEER_TOPK)
        b = _top_rows(s2, PEER_TOPK)
        cand_ref[...] = jnp.full((_NCAND, tb), -jnp.inf, f32)
        r = 0
        for p in range(PEER_TOPK):
            for q in range(_PAIR_COUNTS[p]):
                cand_ref[r:r + 1, :] = a[p] + b[q]
                r += 1
        top = _top_rows(cand_ref[...], PEER_TOPK)
        mx = a[0] + b[0]
        z = jnp.zeros_like(mx)
        for tv in top:
            z = z + jnp.exp(tv - mx)
        s1_ref[h] = s1
        s2_ref[h] = s2
        a1_ref[h] = jnp.exp(s1 - a[0]) / z
        e2_ref[h] = jnp.exp(s2 - b[0])
        thr_ref[h:h + 1, :] = top[PEER_TOPK - 1]


def _peer_route(qp, k1, k2, tb=512):
    n = qp.shape[0]
    big = lambda: pl.BlockSpec((PEER_HEADS, PEER_NKEYS, tb), lambda i: (0, 0, i))
    bigs = jax.ShapeDtypeStruct((PEER_HEADS, PEER_NKEYS, n), f32)
    return pl.pallas_call(
        _peer_route_kernel,
        out_shape=(bigs, bigs, bigs, bigs, jax.ShapeDtypeStruct((PEER_HEADS, n), f32)),
        grid=(n // tb,),
        in_specs=[pl.BlockSpec((tb, qp.shape[1]), lambda i: (i, 0)),
                  pl.BlockSpec(k1.shape, lambda i: (0, 0, 0)),
                  pl.BlockSpec(k2.shape, lambda i: (0, 0, 0))],
        out_specs=(big(), big(), big(), big(), pl.BlockSpec((PEER_HEADS, tb), lambda i: (0, i))),
        scratch_shapes=[pltpu.VMEM((_NCAND, tb), f32)],
        compiler_params=_cparams(("parallel",)),
        name="peer_route",
    )(qp, k1, k2)


PEER_EB = 1024


def _peer_dense_kernel(xn_ref, u_ref, vt_ref, s1_ref, s2_ref, a1_ref, e2_ref, thr_ref, x1_ref, gf_ref,
                       o_ref, acc_ref, act_ref, *, final_norm):
    e = pl.program_id(1)
    tb = xn_ref.shape[0]

    @pl.when(e == 0)
    def _():
        acc_ref[...] = jnp.zeros_like(acc_ref)

    ht = lax.dot_general(u_ref[...], xn_ref[...], (((1,), (1,)), ((), ())), preferred_element_type=f32)
    for ii in range(PEER_EB // PEER_NKEYS):
        i = e * (PEER_EB // PEER_NKEYS) + ii
        g = jnp.zeros((PEER_NKEYS, tb), f32)
        for h in range(PEER_HEADS):
            c = s1_ref[h, pl.ds(i, 1), :]
            a = a1_ref[h, pl.ds(i, 1), :]
            hit = (s2_ref[h] + c) >= thr_ref[h:h + 1, :]
            g = g + jnp.where(hit, e2_ref[h] * a, 0.0)
        hh = ht[PEER_NKEYS * ii:PEER_NKEYS * (ii + 1), :]
        act = 0.5 * hh * (1.0 + lax.erf(hh * (2.0 ** -0.5)))
        act_ref[PEER_NKEYS * ii:PEER_NKEYS * (ii + 1), :] = (act * g).astype(bf16)
    acc_ref[...] += jnp.dot(vt_ref[...], act_ref[...], preferred_element_type=f32)

    @pl.when(e == pl.num_programs(1) - 1)
    def _():
        y = x1_ref[...] + acc_ref[...].T
        if final_norm:
            ms = jnp.mean(y * y, axis=-1, keepdims=True)
            y = y * lax.rsqrt(ms + EPS) * gf_ref[...]
        o_ref[...] = y


def _peer_dense(xn, u, vt, s1, s2, a1, e2, thr, x1, gf, final_norm, tb=256):
    n = xn.shape[0]
    ne = u.shape[0]
    big = lambda: pl.BlockSpec((PEER_HEADS, PEER_NKEYS, tb), lambda t, e: (0, 0, t))
    return pl.pallas_call(
        functools.partial(_peer_dense_kernel, final_norm=final_norm),
        out_shape=jax.ShapeDtypeStruct((n, D_MODEL), f32),
        grid=(n // tb, ne // PEER_EB),
        in_specs=[pl.BlockSpec((tb, D_MODEL), lambda t, e: (t, 0)),
                  pl.BlockSpec((PEER_EB, D_MODEL), lambda t, e: (e, 0)),
                  pl.BlockSpec((D_MODEL, PEER_EB), lambda t, e: (0, e)),
                  big(), big(), big(), big(),
                  pl.BlockSpec((PEER_HEADS, tb), lambda t, e: (0, t)),
                  pl.BlockSpec((tb, D_MODEL), lambda t, e: (t, 0)),
                  pl.BlockSpec((1, D_MODEL), lambda t, e: (0, 0))],
        out_specs=pl.BlockSpec((tb, D_MODEL), lambda t, e: (t, 0)),
        scratch_shapes=[pltpu.VMEM((D_MODEL, tb), f32), pltpu.VMEM((PEER_EB, tb), bf16)],
        compiler_params=_cparams(("parallel", "arbitrary")),
        name="peer_dense",
    )(xn, u, vt, s1, s2, a1, e2, thr, x1, gf)


def _pack_w_in(w):
    offs = np.concatenate([[0], np.cumsum(np.array(IN_SIZES))]).tolist()
    seg = lambda k: w[:, offs[k]:offs[k + 1]]
    hq, hf, hi, hog, aq, ak, av, iq, ik, iw, ga, gb = [seg(k) for k in range(12)]
    pad = jnp.zeros((w.shape[0], 64 - IDX_HEADS), w.dtype)
    return jnp.concatenate([hq, hf, hi, hog, aq, iq, ak, av, ik, iw, pad, ga, gb], axis=1).astype(bf16)


def _layer(x2, batch, seq, norm_mix, w_in, lb, hg_norm, ikg, ikb, bias_tiles, w_up_a, w_up_b, w_out,
           norm_ffn, peer_wq, peer_keys, peer_u, peer_v):
    proj = _inproj(x2, norm_mix.reshape(1, -1), _pack_w_in(w_in))
    iq, ak, ik, avt, wt = _dsa_prep(proj, ikg.reshape(1, -1), ikb.reshape(1, -1))
    ya = _hgrn2(proj, lb.reshape(1, -1), hg_norm.reshape(1, -1), batch, seq)
    yb = _dsa(proj, iq, wt, ik, ak, avt, bias_tiles, batch, seq)
    x1, xn, qp = _merge(x2, ya, yb, proj, w_up_a.astype(bf16), w_up_b.astype(bf16), w_out.astype(bf16),
                        norm_ffn.reshape(1, -1), peer_wq.astype(bf16))
    s1, s2, a1, e2, thr = _peer_route(qp, peer_keys[0].astype(bf16), peer_keys[1].astype(bf16))
    return xn, s1, s2, a1, e2, thr, x1


def kernel(x, norm_mix, w_in, hg_lb, hg_norm, idx_k_norm_g, idx_k_norm_b, rel_bias, w_up_a, w_up_b, w_out,
           norm_ffn, peer_wq, peer_keys, peer_u, peer_v, norm_final):
    batch, seq, _ = x.shape
    depth = w_in.shape[0]
    lb_all = jnp.cumsum(jax.nn.softmax(hg_lb.astype(f32), axis=0), axis=0)
    bias_tiles = _rel_bias_tiles(rel_bias)
    x2 = x.reshape(batch * seq, D_MODEL)
    gf = norm_final.reshape(1, -1)
    for l in range(depth):
        xn, s1, s2, a1, e2, thr, x1 = _layer(
            x2, batch, seq, norm_mix[l], w_in[l], lb_all[l], hg_norm[l], idx_k_norm_g[l], idx_k_norm_b[l],
            bias_tiles, w_up_a[l], w_up_b[l], w_out[l], norm_ffn[l], peer_wq[l], peer_keys[l], peer_u[l], peer_v[l])
        x2 = _peer_dense(xn, peer_u[l].astype(bf16), peer_v[l].astype(bf16).T, s1, s2, a1, e2, thr, x1, gf,
                         final_norm=(l == depth - 1))
    return x2.reshape(batch, seq, D_MODEL)
```

```python
import functools
import math

import jax
import jax.numpy as jnp
import numpy as np
from jax import lax
from jax.experimental import pallas as pl
from jax.experimental.pallas import tpu as pltpu

f32 = jnp.float32
bf16 = jnp.bfloat16
i32 = jnp.int32

D_MODEL = 1024
EPS = 1e-6
HG_HEADS, HG_D = 4, 128
HG_W = HG_HEADS * HG_D
ATT_HEADS, ATT_DH = 8, 64
ATT_W = ATT_HEADS * ATT_DH
IDX_HEADS, IDX_DH = 4, 64
TOPK_MAX = 256
REL_BUCKETS, REL_MAX_DIST = 32, 128
PEER_HEADS, PEER_NKEYS, PEER_HALF, PEER_TOPK = 8, 128, 128, 16
PEER_EXPERTS = PEER_NKEYS * PEER_NKEYS
IN_SIZES = (HG_W, HG_W, HG_W, HG_W, ATT_W, ATT_DH, ATT_DH, IDX_HEADS * IDX_DH, IDX_DH, IDX_HEADS, D_MODEL, D_MODEL)

LANES = 128
BF16_SUBLANES = 16
VMEM_LIMIT = 56 * 1024 * 1024

COL_BLK = 512
PROJ_W = 10 * COL_BLK
CB_HQ, CB_HF, CB_HI, CB_HOG, CB_AQ, CB_IDX = 0, 1, 2, 3, 4, 5
CB_GA, CB_GB = 3, 4
NEG = -1e30


def _cparams(sem):
    return pltpu.CompilerParams(dimension_semantics=sem, vmem_limit_bytes=VMEM_LIMIT)


def _inproj_kernel(x_ref, g_ref, w_ref, o_ref, xn_ref):
    @pl.when(pl.program_id(1) == 0)
    def _():
        x = x_ref[...]
        ms = jnp.mean(x * x, axis=-1, keepdims=True)
        xn_ref[...] = (x * lax.rsqrt(ms + EPS) * g_ref[...]).astype(bf16)

    o_ref[...] = jnp.dot(xn_ref[...], w_ref[...], preferred_element_type=f32)


def _inproj(x2, g, w, tm=1024):
    n = x2.shape[0]
    return pl.pallas_call(
        _inproj_kernel,
        out_shape=jax.ShapeDtypeStruct((n, PROJ_W), f32),
        grid=(n // tm, PROJ_W // COL_BLK),
        in_specs=[
            pl.BlockSpec((tm, D_MODEL), lambda i, j: (i, 0)),
            pl.BlockSpec((1, D_MODEL), lambda i, j: (0, 0)),
            pl.BlockSpec((D_MODEL, COL_BLK), lambda i, j: (0, j)),
        ],
        out_specs=pl.BlockSpec((tm, COL_BLK), lambda i, j: (i, j)),
        scratch_shapes=[pltpu.VMEM((tm, D_MODEL), bf16)],
        compiler_params=_cparams(("parallel", "arbitrary")),
        name="inproj",
    )(x2, g, w)


DSA_T = 128


def _dsa_prep_kernel(p_ref, g_ref, b_ref, iq_ref, ak_ref, ik_ref, avt_ref, wt_ref):
    t = DSA_T
    p = p_ref[...]
    iq_ref[...] = p[:, 0:256].astype(bf16)
    ak_ref[...] = p[:, 256:320].astype(bf16)
    ik = p[:, 384:448]
    mu = jnp.mean(ik, axis=-1, keepdims=True)
    var = jnp.mean(jnp.square(ik - mu), axis=-1, keepdims=True)
    y = (ik - mu) * lax.rsqrt(var + EPS) * g_ref[...] + b_ref[...]
    ik_ref[...] = y.astype(bf16)
    kv_t = p[:, 256:384].T
    for c in range(p.shape[0] // t):
        avt_ref[c] = kv_t[64:128, t * c:t * (c + 1)].astype(bf16)
    wt_ref[...] = p[:, 384:512].T[64:72, :] * (IDX_HEADS ** -0.5 * IDX_DH ** -0.5)


def _dsa_prep(proj, g, b, tm=1024):
    n = proj.shape[0]
    t = DSA_T
    row = lambda w: pl.BlockSpec((tm, w), lambda i: (i, 0))
    return pl.pallas_call(
        _dsa_prep_kernel,
        out_shape=(
            jax.ShapeDtypeStruct((n, 256), bf16),
            jax.ShapeDtypeStruct((n, 64), bf16),
            jax.ShapeDtypeStruct((n, 64), bf16),
            jax.ShapeDtypeStruct((n // t, 64, t), bf16),
            jax.ShapeDtypeStruct((8, n), f32),
        ),
        grid=(n // tm,),
        in_specs=[
            pl.BlockSpec((tm, COL_BLK), lambda i: (i, CB_IDX)),
            pl.BlockSpec((1, 64), lambda i: (0, 0)),
            pl.BlockSpec((1, 64), lambda i: (0, 0)),
        ],
        out_specs=(row(256), row(64), row(64),
                   pl.BlockSpec((tm // t, 64, t), lambda i: (i, 0, 0)),
                   pl.BlockSpec((8, tm), lambda i: (0, i))),
        compiler_params=_cparams(("parallel",)),
        name="dsa_prep",
    )(proj, g, b)


HG_CHUNK = 32


def _split3(a):
    a1 = a.astype(bf16)
    r1 = a - a1.astype(f32)
    a2 = r1.astype(bf16)
    r2 = r1 - a2.astype(f32)
    return a1, a2, r2.astype(bf16)


def _hgrn2_kernel(q_ref, f_ref, i_ref, og_ref, lb_ref, gn_ref, o_ref, st_ref, oacc_ref, *, ts):
    c = HG_CHUNK

    @pl.when(pl.program_id(1) == 0)
    def _():
        st_ref[...] = jnp.zeros_like(st_ref)

    r_io = lax.broadcasted_iota(i32, (c, c), 0)
    c_io = lax.broadcasted_iota(i32, (c, c), 1)
    causal = c_io <= r_io
    tri = causal.astype(bf16)
    mid = c // 2 - 1

    for h in range(HG_HEADS):
        sl = slice(HG_D * h, HG_D * (h + 1))
        lb = lb_ref[:, sl]

        def chunk(ci, carry, sl=sl, lb=lb, h=h):
            rows = pl.ds(pl.multiple_of(ci * c, c), c)
            qr = q_ref[rows, sl]
            q = qr * jax.nn.sigmoid(qr)
            fg = lb + (1.0 - lb) * jax.nn.sigmoid(f_ref[rows, sl])
            k = 1.0 - fg
            v = i_ref[rows, sl].astype(bf16)
            l1, l2, l3 = _split3(jnp.log(fg))
            b = (jnp.dot(tri, l1, preferred_element_type=f32)
                 + jnp.dot(tri, l2, preferred_element_type=f32)
                 + jnp.dot(tri, l3, preferred_element_type=f32))
            b_mid = b[mid:mid + 1, :]
            b_last = b[c - 1:c, :]
            qt = (q * jnp.exp(b - b_mid)).astype(bf16)
            kt = (k * jnp.exp(b_mid - b)).astype(bf16)
            a = lax.dot_general(qt, kt, (((1,), (1,)), ((), ())), preferred_element_type=f32)
            a = jnp.where(causal, a, 0.0).astype(bf16)
            intra = jnp.dot(a, v, preferred_element_type=f32)
            st = st_ref[h]
            qe = (q * jnp.exp(b)).astype(bf16)
            inter = lax.dot_general(qe, st.astype(bf16), (((1,), (1,)), ((), ())), preferred_element_type=f32)
            oacc_ref[rows, sl] = inter + intra
            ke = (k * jnp.exp(b_last - b)).astype(bf16)
            upd = lax.dot_general(v, ke, (((0,), (0,)), ((), ())), preferred_element_type=f32)
            st_ref[h] = st * jnp.exp(b_last) + upd
            return carry

        lax.fori_loop(0, ts // c, chunk, 0)

    for h in range(HG_HEADS):
        sl = slice(HG_D * h, HG_D * (h + 1))
        o = oacc_ref[:, sl]
        ms = jnp.mean(o * o, axis=-1, keepdims=True)
        og = og_ref[:, sl]
        y = o * lax.rsqrt(ms + EPS) * gn_ref[:, sl] * (og * jax.nn.sigmoid(og))
        o_ref[:, sl] = y.astype(o_ref.dtype)


def _hgrn2(proj, lb, gn, batch, seq, ts=256):
    n = proj.shape[0]
    nsb = seq // ts
    col = lambda cb: pl.BlockSpec((ts, COL_BLK), lambda b, s, cb=cb: (b * nsb + s, cb))
    return pl.pallas_call(
        functools.partial(_hgrn2_kernel, ts=ts),
        out_shape=jax.ShapeDtypeStruct((n, HG_W), bf16),
        grid=(batch, nsb),
        in_specs=[col(CB_HQ), col(CB_HF), col(CB_HI), col(CB_HOG),
                  pl.BlockSpec((1, HG_W), lambda b, s: (0, 0)),
                  pl.BlockSpec((1, HG_W), lambda b, s: (0, 0))],
        out_specs=pl.BlockSpec((ts, HG_W), lambda b, s: (b * nsb + s, 0)),
        scratch_shapes=[pltpu.VMEM((HG_HEADS, HG_D, HG_D), f32), pltpu.VMEM((ts, HG_W), f32)],
        compiler_params=_cparams(("parallel", "arbitrary")),
        name="hgrn2",
    )(proj, proj, proj, proj, lb, gn)


LOG2E = 1.4426950408889634
FLT_MAX = 3.4028234663852886e38
NT_DIMS = (((1,), (1,)), ((), ()))


def _key_to_float(key):
    bits = jnp.where(key >= 0, key, key ^ jnp.int32(0x7FFFFFFF))
    return pltpu.bitcast(bits, f32)


def _dsa_kernel(iq_ref, wt_ref, aq_ref, ik_ref, ak_ref, avt_ref, bias_ref, o_ref,
                sc_ref, iqs_ref, qs_ref, lg_ref, pt_ref, al_ref, acc_ref, *, ktop):
    t = DSA_T
    qi = pl.program_id(1)
    npair = (qi + 2) // 2
    krow = lax.broadcasted_iota(i32, (t, t), 0)
    qcol = lax.broadcasted_iota(i32, (t, t), 1)
    qpos = qi * t + lax.broadcasted_iota(i32, (1, t), 1)

    iq = iq_ref[...]
    for j in range(IDX_HEADS):
        iqs_ref[t * j:t * (j + 1), :] = iq[:, IDX_DH * j:IDX_DH * (j + 1)]
    aq = aq_ref[...] * (ATT_DH ** -0.5 * LOG2E)
    for h in range(ATT_HEADS):
        qs_ref[t * h:t * (h + 1), :] = aq[:, ATT_DH * h:ATT_DH * (h + 1)].astype(bf16)
    wt = wt_ref[...]

    def score_pair(kp):
        ksl = pl.ds(pl.multiple_of(kp * 2 * t, 2 * t), 2 * t)
        rel = lax.dot_general(ik_ref[ksl, :], iqs_ref[...], NT_DIMS, preferred_element_type=f32)
        for u in range(2):
            kb = 2 * kp + u
            s = jnp.zeros((t, t), f32)
            for j in range(IDX_HEADS):
                s = s + wt[j:j + 1, :] * jnp.maximum(rel[t * u:t * (u + 1), t * j:t * (j + 1)], 0.0)
            sc_ref[kb] = jnp.where((kb * t + krow) <= (qi * t + qcol), s, -jnp.inf)

    def score_quad(kq, carry):
        score_pair(2 * kq)
        score_pair(2 * kq + 1)
        return carry

    lax.fori_loop(0, (qi + 4) // 4, score_quad, 0)

    def count_keys(pred):
        def body(kp, acc):
            acc = acc + pred(sc_ref[2 * kp], 2 * kp).astype(i32)
            return acc + pred(sc_ref[2 * kp + 1], 2 * kp + 1).astype(i32)
        acc = lax.fori_loop(0, npair, body, jnp.zeros((t, t), i32))
        return jnp.sum(acc, axis=0, keepdims=True)

    take_all = qpos < ktop

    def unsettled(cnt):
        return jnp.max(jnp.where(take_all, ktop, cnt)) > ktop

    bits_per_check = 4

    def bit_cond(c):
        return (c[0] < 32 // bits_per_check) & unsettled(c[2])

    def bit_step(c):
        g, key, cnt = c
        for b in range(bits_per_check):
            cand = key + jnp.left_shift(jnp.int32(1), 31 - b - bits_per_check * g)
            cf = _key_to_float(cand)
            cc = count_keys(lambda s, kb: s >= cf)
            ok = cc >= ktop
            key = jnp.where(ok, cand, key)
            cnt = jnp.where(ok, cc, cnt)
        return g + 1, key, cnt

    _, key, cnt = lax.while_loop(bit_cond, bit_step,
                                 (jnp.int32(0), jnp.full((1, t), -2 ** 31, i32), qpos + 1))
    thr = jnp.where(take_all, -FLT_MAX, _key_to_float(key))

    @pl.when(unsettled(cnt))
    def _():
        need = ktop - count_keys(lambda s, kb: s > thr)

        def cstep(i, cut):
            cand = cut + jnp.left_shift(jnp.int32(1), 12 - i)
            c = count_keys(lambda s, kb: (s == thr) & ((krow + kb * t) < cand))
            return jnp.where(c < need, cand, cut)

        cut = lax.fori_loop(0, 13, cstep, jnp.zeros((1, t), i32))

        def drop(kb, carry):
            s = sc_ref[kb]
            sc_ref[kb] = jnp.where((s == thr) & ((krow + kb * t) > cut), -jnp.inf, s)
            return carry

        lax.fori_loop(0, 2 * npair, drop, 0)

    nq = sc_ref.shape[0]
    hw = ATT_HEADS * t
    acc_ref[...] = jnp.zeros(acc_ref.shape, f32)
    pt_ref[1] = jnp.zeros(pt_ref.shape[1:], bf16)
    al_ref[1] = jnp.ones(al_ref.shape[1:], f32)

    def logit_stage(j, par):
        for u in range(2):
            kb = jnp.minimum(2 * j + u, nq - 1)
            kblk = ak_ref[pl.ds(pl.multiple_of(kb * t, t), t), :]
            lg_ref[par, u] = lax.dot_general(kblk, qs_ref[...], NT_DIMS, preferred_element_type=f32)

    def softmax_stage(j, par, m_all, l_all):
        for u in range(2):
            kb = 2 * j + u
            madd = jnp.where(sc_ref[kb] >= thr, 0.0, NEG)
            near = jnp.clip(qi - kb, 0, 2)
            ms, ls, als = [], [], []
            for h in range(ATT_HEADS):
                hs = slice(t * h, t * (h + 1))
                lg = lg_ref[par, u, :, hs] + (madd + bias_ref[near, h])
                m_old = m_all[:, hs]
                m_new = jnp.maximum(m_old, jnp.max(lg, axis=0, keepdims=True))
                alpha = jnp.exp2(m_old - m_new)
                p = jnp.exp2(lg - m_new)
                ls.append(alpha * l_all[:, hs] + jnp.sum(p, axis=0, keepdims=True))
                ms.append(m_new)
                als.append(alpha)
                pt_ref[par, u, :, hs] = p.astype(bf16)
            al_ref[par, u] = jnp.concatenate(als, axis=1)
            m_all, l_all = jnp.concatenate(ms, axis=1), jnp.concatenate(ls, axis=1)
        return m_all, l_all

    def value_stage(j, par):
        for u in range(2):
            kb = jnp.maximum(2 * j + u, 0)
            pv = jnp.dot(avt_ref[kb], pt_ref[par, u], preferred_element_type=f32)
            acc_ref[...] = acc_ref[...] * al_ref[par, u] + pv

    def attn_trip(q, carry):
        for par in range(2):
            j = 2 * q + par
            value_stage(j - 1, 1 - par)
            carry = softmax_stage(j, par, *carry)
            logit_stage(j + 1, 1 - par)
        return carry

    nquad = (qi + 4) // 4
    logit_stage(0, 0)
    _, l_all = lax.fori_loop(0, nquad, attn_trip, (jnp.full((1, hw), NEG, f32), jnp.zeros((1, hw), f32)))
    value_stage(2 * nquad - 1, 1)

    out_t = acc_ref[...] / l_all
    out_t = jnp.concatenate([out_t[:, t * h:t * (h + 1)] for h in range(ATT_HEADS)], axis=0)
    o_ref[...] = out_t.T.astype(o_ref.dtype)


def _rel_bias_tiles(rel_bias):
    t = DSA_T
    ki = jnp.arange(t)[:, None]
    qj = jnp.arange(t)[None, :]
    far = rel_bias[REL_BUCKETS - 1].astype(f32)
    tiles = []
    for v in range(2):
        dist = jnp.maximum(v * t + qj - ki, 0)
        max_exact = REL_BUCKETS // 2
        nf = jnp.maximum(dist, 1).astype(f32)
        large = max_exact + (jnp.log(nf / max_exact) / math.log(REL_MAX_DIST / max_exact)
                             * (REL_BUCKETS - max_exact)).astype(i32)
        large = jnp.minimum(large, REL_BUCKETS - 1)
        bucket = jnp.where(dist < max_exact, dist, large)
        tiles.append(jnp.transpose((rel_bias[bucket].astype(f32) - far) * LOG2E, (2, 0, 1)))
    tiles.append(jnp.zeros_like(tiles[0]))
    return jnp.stack(tiles)


def _dsa(proj, iq, wt, ik, ak, avt, bias_tiles, batch, seq):
    n = proj.shape[0]
    t = DSA_T
    nq = seq // t
    assert nq % 4 == 0 and 2 * t >= REL_MAX_DIST
    ktop = min(TOPK_MAX, seq // 4)
    qrow = lambda width: pl.BlockSpec((t, width), lambda b, q: (b * nq + q, 0))
    kv = pl.BlockSpec((seq, 64), lambda b, q: (b, 0))
    hw = ATT_HEADS * t
    return pl.pallas_call(
        functools.partial(_dsa_kernel, ktop=ktop),
        out_shape=jax.ShapeDtypeStruct((n, ATT_W), bf16),
        grid=(batch, nq),
        in_specs=[qrow(256),
                  pl.BlockSpec((8, t), lambda b, q: (0, b * nq + q)),
                  pl.BlockSpec((t, COL_BLK), lambda b, q: (b * nq + q, CB_AQ)),
                  kv, kv,
                  pl.BlockSpec((nq, 64, t), lambda b, q: (b, 0, 0)),
                  pl.BlockSpec((3, ATT_HEADS, t, t), lambda b, q: (0, 0, 0, 0))],
        out_specs=qrow(ATT_W),
        scratch_shapes=[pltpu.VMEM((nq, t, t), f32),
                        pltpu.VMEM((IDX_HEADS * t, IDX_DH), bf16),
                        pltpu.VMEM((hw, ATT_DH), bf16),
                        pltpu.VMEM((2, 2, t, hw), f32),
                        pltpu.VMEM((2, 2, t, hw), bf16),
                        pltpu.VMEM((2, 2, 1, hw), f32),
                        pltpu.VMEM((ATT_DH, hw), f32)],
        compiler_params=_cparams(("parallel", "arbitrary")),
        name="dsa",
    )(iq, wt, proj, ik, ak, avt, bias_tiles)


def _merge_kernel(x_ref, ya_ref, yb_ref, ga_ref, gb_ref, wa_ref, wb_ref, wo_ref, g2_ref, wq_ref,
                  x1_ref, xn_ref, qp_ref):
    ha = jnp.dot(ya_ref[...], wa_ref[...], preferred_element_type=f32)
    hb = jnp.dot(yb_ref[...], wb_ref[...], preferred_element_type=f32)
    h = jax.nn.sigmoid(ga_ref[...]) * ha + jax.nn.sigmoid(gb_ref[...]) * hb
    x1 = x_ref[...] + jnp.dot(h.astype(bf16), wo_ref[...], preferred_element_type=f32)
    x1_ref[...] = x1
    ms = jnp.mean(x1 * x1, axis=-1, keepdims=True)
    xn = (x1 * lax.rsqrt(ms + EPS) * g2_ref[...]).astype(bf16)
    xn_ref[...] = xn
    qp_ref[...] = jnp.dot(xn, wq_ref[...], preferred_element_type=f32).astype(bf16)


def _merge(x2, ya, yb, proj, wa, wb, wo, g2, wq, tm=512):
    n = x2.shape[0]
    qw = wq.shape[1]
    row = lambda width: pl.BlockSpec((tm, width), lambda i: (i, 0))
    full = lambda a: pl.BlockSpec(a.shape, lambda i: (0, 0))
    return pl.pallas_call(
        _merge_kernel,
        out_shape=(jax.ShapeDtypeStruct((n, D_MODEL), f32),
                   jax.ShapeDtypeStruct((n, D_MODEL), bf16),
                   jax.ShapeDtypeStruct((n, qw), bf16)),
        grid=(n // tm,),
        in_specs=[row(D_MODEL), row(HG_W), row(ATT_W),
                  pl.BlockSpec((tm, D_MODEL), lambda i: (i, CB_GA)),
                  pl.BlockSpec((tm, D_MODEL), lambda i: (i, CB_GB)),
                  full(wa), full(wb), full(wo), full(g2), full(wq)],
        out_specs=(row(D_MODEL), row(D_MODEL), row(qw)),
        compiler_params=_cparams(("parallel",)),
        name="merge",
    )(x2, ya, yb, proj, proj, wa, wb, wo, g2, wq)


_PAIR_COUNTS = tuple(PEER_TOPK // (p + 1) for p in range(PEER_TOPK))
_NCAND = 56


def _top_rows(s, nrows):
    out = []
    rowi = lax.broadcasted_iota(i32, s.shape, 0)
    rank = jnp.full(s.shape, float(nrows), f32)
    for r in range(nrows):
        m = jnp.max(s, axis=0, keepdims=True)
        out.append(m)
        first = jnp.min(jnp.where(s == m, rowi, s.shape[0]), axis=0, keepdims=True)
        hit = rowi == first
        s = jnp.where(hit, -jnp.inf, s)
        rank = jnp.where(hit, float(r), rank)
    return out, rank


def _peer_route_kernel(qp_ref, k1_ref, k2_ref, n1_ref, a1_ref, r2_ref, e2_ref, cand_ref, top_ref):
    tb = qp_ref.shape[0]
    for h in range(PEER_HEADS):
        q1 = qp_ref[:, 2 * PEER_HALF * h:2 * PEER_HALF * h + PEER_HALF]
        q2 = qp_ref[:, 2 * PEER_HALF * h + PEER_HALF:2 * PEER_HALF * (h + 1)]
        s1 = lax.dot_general(k1_ref[h], q1, NT_DIMS, preferred_element_type=f32)
        s2 = lax.dot_general(k2_ref[h], q2, NT_DIMS, preferred_element_type=f32)
        a, rank1 = _top_rows(s1, PEER_TOPK)
        b, rank2 = _top_rows(s2, PEER_TOPK)
        cand_ref[...] = jnp.full((_NCAND, tb), -jnp.inf, f32)
        r = 0
        for p in range(PEER_TOPK):
            top_ref[p:p + 1, :] = a[p]
            for q in range(_PAIR_COUNTS[p]):
                cand_ref[r:r + 1, :] = a[p] + b[q]
                r += 1
        top, _ = _top_rows(cand_ref[...], PEER_TOPK)
        thr = top[PEER_TOPK - 1]
        mx = a[0] + b[0]
        z = jnp.zeros_like(mx)
        for tv in top:
            z = z + jnp.exp(tv - mx)
        a16 = top_ref[...]
        n16 = jnp.zeros((PEER_TOPK, tb), f32)
        for q in range(PEER_TOPK):
            n16 = n16 + jnp.where(a16 + b[q] >= thr, 1.0, 0.0)
        top_ref[...] = n16
        n1 = jnp.zeros((PEER_NKEYS, tb), f32)
        for p in range(PEER_TOPK):
            n1 = jnp.where(rank1 == float(p), top_ref[p:p + 1, :], n1)
        n1_ref[h] = n1
        a1_ref[h] = jnp.exp(s1 - a[0]) / z
        r2_ref[h] = rank2.astype(bf16)
        e2_ref[h] = jnp.exp(s2 - b[0]).astype(bf16)


def _peer_route(qp, k1, k2, tb=512):
    n = qp.shape[0]
    big = lambda: pl.BlockSpec((PEER_HEADS, PEER_NKEYS, tb), lambda i: (0, 0, i))
    bigs = lambda dt: jax.ShapeDtypeStruct((PEER_HEADS, PEER_NKEYS, n), dt)
    return pl.pallas_call(
        _peer_route_kernel,
        out_shape=(bigs(f32), bigs(f32), bigs(bf16), bigs(bf16)),
        grid=(n // tb,),
        in_specs=[pl.BlockSpec((tb, qp.shape[1]), lambda i: (i, 0)),
                  pl.BlockSpec(k1.shape, lambda i: (0, 0, 0)),
                  pl.BlockSpec(k2.shape, lambda i: (0, 0, 0))],
        out_specs=(big(), big(), big(), big()),
        scratch_shapes=[pltpu.VMEM((_NCAND, tb), f32), pltpu.VMEM((PEER_TOPK, tb), f32)],
        compiler_params=_cparams(("parallel",)),
        name="peer_route",
    )(qp, k1, k2)


PEER_EB = 1024


def _peer_dense_kernel(xn_ref, u_ref, vt_ref, n1_ref, a1_ref, r2_ref, e2_ref, x1_ref, gf_ref,
                       o_ref, acc_ref, act_ref, g_ref, *, final_norm):
    e = pl.program_id(1)
    tb = xn_ref.shape[0]

    @pl.when(e == 0)
    def _():
        acc_ref[...] = jnp.zeros_like(acc_ref)

    nblk = PEER_EB // PEER_NKEYS
    reps = PEER_NKEYS // BF16_SUBLANES

    def gates(blk, slot):
        for ii in range(nblk):
            i = jnp.minimum(blk * nblk + ii, PEER_NKEYS - 1)
            g = jnp.zeros((PEER_NKEYS, tb), bf16)
            for h in range(PEER_HEADS):
                n_row = jnp.broadcast_to(n1_ref[h, pl.ds(i, 1), :], (BF16_SUBLANES, tb)).astype(bf16)
                a_row = jnp.broadcast_to(a1_ref[h, pl.ds(i, 1), :], (BF16_SUBLANES, tb)).astype(bf16)
                n_all = jnp.tile(n_row, (reps, 1))
                a_all = jnp.tile(a_row, (reps, 1))
                g = g + jnp.where(r2_ref[h] < n_all, e2_ref[h] * a_all, jnp.zeros((), bf16))
            g_ref[slot, PEER_NKEYS * ii:PEER_NKEYS * (ii + 1), :] = g

    half = PEER_EB // 2

    def hidden(slot):
        base = PEER_EB * slot
        return [lax.dot_general(u_ref[base + half * k:base + half * (k + 1), :], xn_ref[...], NT_DIMS,
                                preferred_element_type=f32) for k in range(2)]

    def outputs(slot, hts):
        base = PEER_EB * slot
        for k in range(2):
            hh = hts[k]
            act = 0.5 * hh * (1.0 + lax.erf(hh * (2.0 ** -0.5)))
            rows = slice(half * k, half * (k + 1))
            act_ref[slot, rows, :] = (act * g_ref[slot, rows, :].astype(f32)).astype(bf16)
            acc_ref[...] += jnp.dot(vt_ref[:, base + half * k:base + half * (k + 1)], act_ref[slot, rows, :],
                                    preferred_element_type=f32)

    @pl.when(e == 0)
    def _():
        gates(1, 1)

    gates(2 * e, 0)
    hts0 = hidden(0)
    hts1 = hidden(1)
    outputs(0, hts0)
    outputs(1, hts1)
    gates(2 * e + 3, 1)

    @pl.when(e == pl.num_programs(1) - 1)
    def _():
        y = x1_ref[...] + acc_ref[...].T
        if final_norm:
            ms = jnp.mean(y * y, axis=-1, keepdims=True)
            y = y * lax.rsqrt(ms + EPS) * gf_ref[...]
        o_ref[...] = y


def _peer_dense(xn, u, vt, n1, a1, r2, e2, x1, gf, final_norm, tb=256):
    n = xn.shape[0]
    ne = u.shape[0]
    big = lambda: pl.BlockSpec((PEER_HEADS, PEER_NKEYS, tb), lambda t, e: (0, 0, t))
    return pl.pallas_call(
        functools.partial(_peer_dense_kernel, final_norm=final_norm),
        out_shape=jax.ShapeDtypeStruct((n, D_MODEL), f32),
        grid=(n // tb, ne // (2 * PEER_EB)),
        in_specs=[pl.BlockSpec((tb, D_MODEL), lambda t, e: (t, 0)),
                  pl.BlockSpec((2 * PEER_EB, D_MODEL), lambda t, e: (e, 0)),
                  pl.BlockSpec((D_MODEL, 2 * PEER_EB), lambda t, e: (0, e)),
                  big(), big(), big(), big(),
                  pl.BlockSpec((tb, D_MODEL), lambda t, e: (t, 0)),
                  pl.BlockSpec((1, D_MODEL), lambda t, e: (0, 0))],
        out_specs=pl.BlockSpec((tb, D_MODEL), lambda t, e: (t, 0)),
        scratch_shapes=[pltpu.VMEM((D_MODEL, tb), f32), pltpu.VMEM((2, PEER_EB, tb), bf16),
                        pltpu.VMEM((2, PEER_EB, tb), bf16)],
        compiler_params=_cparams(("parallel", "arbitrary")),
        name="peer_dense",
    )(xn, u, vt, n1, a1, r2, e2, x1, gf)


def _pack_w_in(w):
    offs = np.concatenate([[0], np.cumsum(np.array(IN_SIZES))]).tolist()
    seg = lambda k: w[:, offs[k]:offs[k + 1]]
    hq, hf, hi, hog, aq, ak, av, iq, ik, iw, ga, gb = [seg(k) for k in range(12)]
    pad = jnp.zeros((w.shape[0], 64 - IDX_HEADS), w.dtype)
    return jnp.concatenate([hq, hf, hi, hog, aq, iq, ak, av, ik, iw, pad, ga, gb], axis=1).astype(bf16)


def _layer(x2, batch, seq, norm_mix, w_in, lb, hg_norm, ikg, ikb, bias_tiles, w_up_a, w_up_b, w_out,
           norm_ffn, peer_wq, peer_keys, peer_u, peer_v):
    proj = _inproj(x2, norm_mix.reshape(1, -1), _pack_w_in(w_in))
    iq, ak, ik, avt, wt = _dsa_prep(proj, ikg.reshape(1, -1), ikb.reshape(1, -1))
    ya = _hgrn2(proj, lb.reshape(1, -1), hg_norm.reshape(1, -1), batch, seq)
    yb = _dsa(proj, iq, wt, ik, ak, avt, bias_tiles, batch, seq)
    x1, xn, qp = _merge(x2, ya, yb, proj, w_up_a.astype(bf16), w_up_b.astype(bf16), w_out.astype(bf16),
                        norm_ffn.reshape(1, -1), peer_wq.astype(bf16))
    route = _peer_route(qp, peer_keys[0].astype(bf16), peer_keys[1].astype(bf16))
    return xn, route, x1


def kernel(x, norm_mix, w_in, hg_lb, hg_norm, idx_k_norm_g, idx_k_norm_b, rel_bias, w_up_a, w_up_b, w_out,
           norm_ffn, peer_wq, peer_keys, peer_u, peer_v, norm_final):
    batch, seq, _ = x.shape
    depth = w_in.shape[0]
    lb_all = jnp.cumsum(jax.nn.softmax(hg_lb.astype(f32), axis=0), axis=0)
    bias_tiles = _rel_bias_tiles(rel_bias)
    x2 = x.reshape(batch * seq, D_MODEL)
    gf = norm_final.reshape(1, -1)
    for l in range(depth):
        xn, route, x1 = _layer(
            x2, batch, seq, norm_mix[l], w_in[l], lb_all[l], hg_norm[l], idx_k_norm_g[l], idx_k_norm_b[l],
            bias_tiles, w_up_a[l], w_up_b[l], w_out[l], norm_ffn[l], peer_wq[l], peer_keys[l], peer_u[l], peer_v[l])
        x2 = _peer_dense(xn, peer_u[l].astype(bf16), peer_v[l].astype(bf16).T, *route, x1, gf,
                         final_norm=(l == depth - 1))
    return x2.reshape(batch, seq, D_MODEL)
```

```python
import functools
import math

import jax
import jax.numpy as jnp
import numpy as np
from jax import lax
from jax.experimental import pallas as pl
from jax.experimental.pallas import tpu as pltpu

f32 = jnp.float32
bf16 = jnp.bfloat16
i32 = jnp.int32

D_MODEL = 1024
EPS = 1e-6
HG_HEADS, HG_D = 4, 128
HG_W = HG_HEADS * HG_D
ATT_HEADS, ATT_DH = 8, 64
ATT_W = ATT_HEADS * ATT_DH
IDX_HEADS, IDX_DH = 4, 64
TOPK_MAX = 256
REL_BUCKETS, REL_MAX_DIST = 32, 128
PEER_HEADS, PEER_NKEYS, PEER_HALF, PEER_TOPK = 8, 128, 128, 16
PEER_EXPERTS = PEER_NKEYS * PEER_NKEYS
IN_SIZES = (HG_W, HG_W, HG_W, HG_W, ATT_W, ATT_DH, ATT_DH, IDX_HEADS * IDX_DH, IDX_DH, IDX_HEADS, D_MODEL, D_MODEL)

LANES = 128
BF16_SUBLANES = 16
VMEM_LIMIT = 56 * 1024 * 1024

COL_BLK = 512
PROJ_W = 10 * COL_BLK
CB_HQ, CB_HF, CB_HI, CB_HOG, CB_AQ, CB_IDX = 0, 1, 2, 3, 4, 5
CB_GA, CB_GB = 3, 4
NEG = -1e30


def _cparams(sem):
    return pltpu.CompilerParams(dimension_semantics=sem, vmem_limit_bytes=VMEM_LIMIT)


def _inproj_kernel(x_ref, g_ref, w_ref, o_ref, xn_ref):
    @pl.when(pl.program_id(1) == 0)
    def _():
        x = x_ref[...]
        ms = jnp.mean(x * x, axis=-1, keepdims=True)
        xn_ref[...] = (x * lax.rsqrt(ms + EPS) * g_ref[...]).astype(bf16)

    o_ref[...] = jnp.dot(xn_ref[...], w_ref[...], preferred_element_type=f32)


def _inproj(x2, g, w, tm=1024):
    n = x2.shape[0]
    return pl.pallas_call(
        _inproj_kernel,
        out_shape=jax.ShapeDtypeStruct((n, PROJ_W), f32),
        grid=(n // tm, PROJ_W // COL_BLK),
        in_specs=[
            pl.BlockSpec((tm, D_MODEL), lambda i, j: (i, 0)),
            pl.BlockSpec((1, D_MODEL), lambda i, j: (0, 0)),
            pl.BlockSpec((D_MODEL, COL_BLK), lambda i, j: (0, j)),
        ],
        out_specs=pl.BlockSpec((tm, COL_BLK), lambda i, j: (i, j)),
        scratch_shapes=[pltpu.VMEM((tm, D_MODEL), bf16)],
        compiler_params=_cparams(("parallel", "arbitrary")),
        name="inproj",
    )(x2, g, w)


DSA_T = 128


def _dsa_prep_kernel(p_ref, g_ref, b_ref, iq_ref, ak_ref, ik_ref, avt_ref, wt_ref):
    t = DSA_T
    p = p_ref[...]
    iq_ref[...] = p[:, 0:256].astype(bf16)
    ak_ref[...] = p[:, 256:320].astype(bf16)
    ik = p[:, 384:448]
    mu = jnp.mean(ik, axis=-1, keepdims=True)
    var = jnp.mean(jnp.square(ik - mu), axis=-1, keepdims=True)
    y = (ik - mu) * lax.rsqrt(var + EPS) * g_ref[...] + b_ref[...]
    ik_ref[...] = y.astype(bf16)
    kv_t = p[:, 256:384].T
    for c in range(p.shape[0] // t):
        avt_ref[c] = kv_t[64:128, t * c:t * (c + 1)].astype(bf16)
    wt_ref[...] = p[:, 384:512].T[64:72, :] * (IDX_HEADS ** -0.5 * IDX_DH ** -0.5)


def _dsa_prep(proj, g, b, tm=1024):
    n = proj.shape[0]
    t = DSA_T
    row = lambda w: pl.BlockSpec((tm, w), lambda i: (i, 0))
    return pl.pallas_call(
        _dsa_prep_kernel,
        out_shape=(
            jax.ShapeDtypeStruct((n, 256), bf16),
            jax.ShapeDtypeStruct((n, 64), bf16),
            jax.ShapeDtypeStruct((n, 64), bf16),
            jax.ShapeDtypeStruct((n // t, 64, t), bf16),
            jax.ShapeDtypeStruct((8, n), f32),
        ),
        grid=(n // tm,),
        in_specs=[
            pl.BlockSpec((tm, COL_BLK), lambda i: (i, CB_IDX)),
            pl.BlockSpec((1, 64), lambda i: (0, 0)),
            pl.BlockSpec((1, 64), lambda i: (0, 0)),
        ],
        out_specs=(row(256), row(64), row(64),
                   pl.BlockSpec((tm // t, 64, t), lambda i: (i, 0, 0)),
                   pl.BlockSpec((8, tm), lambda i: (0, i))),
        compiler_params=_cparams(("parallel",)),
        name="dsa_prep",
    )(proj, g, b)


HG_CHUNK = 32


def _split3(a):
    a1 = a.astype(bf16)
    r1 = a - a1.astype(f32)
    a2 = r1.astype(bf16)
    r2 = r1 - a2.astype(f32)
    return a1, a2, r2.astype(bf16)


def _hgrn2_kernel(q_ref, f_ref, i_ref, og_ref, lb_ref, gn_ref, o_ref, st_ref, oacc_ref, *, ts):
    c = HG_CHUNK

    @pl.when(pl.program_id(1) == 0)
    def _():
        st_ref[...] = jnp.zeros_like(st_ref)

    r_io = lax.broadcasted_iota(i32, (c, c), 0)
    c_io = lax.broadcasted_iota(i32, (c, c), 1)
    causal = c_io <= r_io
    tri = causal.astype(bf16)
    mid = c // 2 - 1

    heads = range(HG_HEADS)
    lanes = [slice(HG_D * h, HG_D * (h + 1)) for h in heads]

    def chunk_pair(cp, carry):
        chains = [(pl.ds(pl.multiple_of((2 * cp + u) * c, c), c), h) for u in range(2) for h in heads]
        q, k, v, b = [], [], [], []
        for rows, h in chains:
            qr = q_ref[rows, lanes[h]]
            q.append(qr * jax.nn.sigmoid(qr))
            lb = lb_ref[:, lanes[h]]
            fg = lb + (1.0 - lb) * jax.nn.sigmoid(f_ref[rows, lanes[h]])
            k.append(1.0 - fg)
            v.append(i_ref[rows, lanes[h]].astype(bf16))
            l1, l2, l3 = _split3(jnp.log(fg))
            b.append(jnp.dot(tri, l1, preferred_element_type=f32)
                     + jnp.dot(tri, l2, preferred_element_type=f32)
                     + jnp.dot(tri, l3, preferred_element_type=f32))
        a = []
        for n in range(len(chains)):
            b_mid = b[n][mid:mid + 1, :]
            qt = (q[n] * jnp.exp(b[n] - b_mid)).astype(bf16)
            kt = (k[n] * jnp.exp(b_mid - b[n])).astype(bf16)
            a.append(lax.dot_general(qt, kt, NT_DIMS, preferred_element_type=f32))
        intra = [jnp.dot(jnp.where(causal, a[n], 0.0).astype(bf16), v[n], preferred_element_type=f32)
                 for n in range(len(chains))]
        for n, (rows, h) in enumerate(chains):
            b_last = b[n][c - 1:c, :]
            st = st_ref[h]
            qe = (q[n] * jnp.exp(b[n])).astype(bf16)
            inter = lax.dot_general(qe, st.astype(bf16), NT_DIMS, preferred_element_type=f32)
            oacc_ref[rows, lanes[h]] = inter + intra[n]
            ke = (k[n] * jnp.exp(b_last - b[n])).astype(bf16)
            upd = lax.dot_general(v[n], ke, (((0,), (0,)), ((), ())), preferred_element_type=f32)
            st_ref[h] = st * jnp.exp(b_last) + upd
        return carry

    lax.fori_loop(0, ts // (2 * c), chunk_pair, 0)

    for h in range(HG_HEADS):
        sl = slice(HG_D * h, HG_D * (h + 1))
        o = oacc_ref[:, sl]
        ms = jnp.mean(o * o, axis=-1, keepdims=True)
        og = og_ref[:, sl]
        y = o * lax.rsqrt(ms + EPS) * gn_ref[:, sl] * (og * jax.nn.sigmoid(og))
        o_ref[:, sl] = y.astype(o_ref.dtype)


def _hgrn2(proj, lb, gn, batch, seq, ts=256):
    n = proj.shape[0]
    nsb = seq // ts
    col = lambda cb: pl.BlockSpec((ts, COL_BLK), lambda b, s, cb=cb: (b * nsb + s, cb))
    return pl.pallas_call(
        functools.partial(_hgrn2_kernel, ts=ts),
        out_shape=jax.ShapeDtypeStruct((n, HG_W), bf16),
        grid=(batch, nsb),
        in_specs=[col(CB_HQ), col(CB_HF), col(CB_HI), col(CB_HOG),
                  pl.BlockSpec((1, HG_W), lambda b, s: (0, 0)),
                  pl.BlockSpec((1, HG_W), lambda b, s: (0, 0))],
        out_specs=pl.BlockSpec((ts, HG_W), lambda b, s: (b * nsb + s, 0)),
        scratch_shapes=[pltpu.VMEM((HG_HEADS, HG_D, HG_D), f32), pltpu.VMEM((ts, HG_W), f32)],
        compiler_params=_cparams(("parallel", "arbitrary")),
        name="hgrn2",
    )(proj, proj, proj, proj, lb, gn)


LOG2E = 1.4426950408889634
FLT_MAX = 3.4028234663852886e38
NT_DIMS = (((1,), (1,)), ((), ()))


def _key_to_float(key):
    bits = jnp.where(key >= 0, key, key ^ jnp.int32(0x7FFFFFFF))
    return pltpu.bitcast(bits, f32)


def _dsa_kernel(iq_ref, wt_ref, aq_ref, ik_ref, ak_ref, avt_ref, bias_ref, o_ref,
                sc_ref, iqs_ref, qs_ref, lg_ref, pt_ref, al_ref, acc_ref, bef_ref, *, ktop):
    t = DSA_T
    qi = pl.program_id(1)
    npair = (qi + 2) // 2
    krow = lax.broadcasted_iota(i32, (t, t), 0)
    qcol = lax.broadcasted_iota(i32, (t, t), 1)
    qpos = qi * t + lax.broadcasted_iota(i32, (1, t), 1)

    iq = iq_ref[...]
    for j in range(IDX_HEADS):
        iqs_ref[t * j:t * (j + 1), :] = iq[:, IDX_DH * j:IDX_DH * (j + 1)]
    aq = aq_ref[...] * (ATT_DH ** -0.5 * LOG2E)
    for h in range(ATT_HEADS):
        qs_ref[t * h:t * (h + 1), :] = aq[:, ATT_DH * h:ATT_DH * (h + 1)].astype(bf16)
    wt = wt_ref[...]

    def score_pair(kp):
        ksl = pl.ds(pl.multiple_of(kp * 2 * t, 2 * t), 2 * t)
        rel = lax.dot_general(ik_ref[ksl, :], iqs_ref[...], NT_DIMS, preferred_element_type=f32)
        for u in range(2):
            kb = 2 * kp + u
            s = jnp.zeros((t, t), f32)
            for j in range(IDX_HEADS):
                s = s + wt[j:j + 1, :] * jnp.maximum(rel[t * u:t * (u + 1), t * j:t * (j + 1)], 0.0)
            sc_ref[kb] = jnp.where((kb * t + krow) <= (qi * t + qcol), s, -jnp.inf)

    def score_quad(kq, carry):
        score_pair(2 * kq)
        score_pair(2 * kq + 1)
        return carry

    lax.fori_loop(0, (qi + 4) // 4, score_quad, 0)

    def count_keys(pred):
        def body(kp, acc):
            acc = acc + pred(sc_ref[2 * kp], 2 * kp).astype(i32)
            return acc + pred(sc_ref[2 * kp + 1], 2 * kp + 1).astype(i32)
        acc = lax.fori_loop(0, npair, body, jnp.zeros((t, t), i32))
        return jnp.sum(acc, axis=0, keepdims=True)

    take_all = qpos < ktop
    few_pos = count_keys(lambda s, kb: s > 0.0) < ktop

    def unsettled(key, cnt):
        done = take_all | (cnt == ktop) | (few_pos & (key == 0))
        return jnp.min(done.astype(i32)) == 0

    bits_per_check = 4

    def bit_cond(c):
        return (c[0] < 32 // bits_per_check) & unsettled(c[1], c[2])

    def bit_step(c):
        g, key, cnt = c
        for b in range(bits_per_check):
            cand = key + jnp.left_shift(jnp.int32(1), 31 - b - bits_per_check * g)
            cf = _key_to_float(cand)
            cc = count_keys(lambda s, kb: s >= cf)
            ok = cc >= ktop
            key = jnp.where(ok, cand, key)
            cnt = jnp.where(ok, cc, cnt)
        return g + 1, key, cnt

    _, key, cnt = lax.while_loop(bit_cond, bit_step,
                                 (jnp.int32(0), jnp.full((1, t), -2 ** 31, i32), qpos + 1))
    thr = jnp.where(take_all, -FLT_MAX, _key_to_float(key))

    @pl.when(jnp.max(jnp.where(take_all, ktop, cnt)) > ktop)
    def _():
        def tally(kb, carry):
            gt, before = carry
            s = sc_ref[kb]
            bef_ref[kb] = before
            ties = jnp.sum(jnp.where(s == thr, 1.0, 0.0), axis=0, keepdims=True)
            return gt + jnp.where(s > thr, 1.0, 0.0), before + ties

        zero = jnp.zeros((1, t), f32)
        nquad = (qi + 4) // 4
        gt, _ = lax.fori_loop(0, 4 * nquad, tally, (jnp.zeros((t, t), f32), zero))
        need = ktop - jnp.sum(gt, axis=0, keepdims=True)
        tri = (qcol <= krow).astype(bf16)

        def drop(kb):
            s = sc_ref[kb]
            tie = s == thr
            run = bef_ref[kb] + jnp.dot(tri, jnp.where(tie, 1.0, 0.0).astype(bf16), preferred_element_type=f32)
            sc_ref[kb] = jnp.where(tie & (run > need), -jnp.inf, s)

        def drop_quad(kq, carry):
            for u in range(4):
                drop(4 * kq + u)
            return carry

        lax.fori_loop(0, nquad, drop_quad, 0)

    nq = sc_ref.shape[0]
    hw = ATT_HEADS * t
    acc_ref[...] = jnp.zeros(acc_ref.shape, f32)
    pt_ref[1] = jnp.zeros(pt_ref.shape[1:], bf16)
    al_ref[1] = jnp.ones(al_ref.shape[1:], f32)

    def logit_stage(j, par):
        for u in range(2):
            kb = jnp.minimum(2 * j + u, nq - 1)
            kblk = ak_ref[pl.ds(pl.multiple_of(kb * t, t), t), :]
            lg_ref[par, u] = lax.dot_general(kblk, qs_ref[...], NT_DIMS, preferred_element_type=f32)

    def softmax_stage(j, par, m_all, l_all):
        for u in range(2):
            kb = 2 * j + u
            madd = jnp.where(sc_ref[kb] >= thr, 0.0, NEG)
            near = jnp.clip(qi - kb, 0, 2)
            ms, ls, als = [], [], []
            for h in range(ATT_HEADS):
                hs = slice(t * h, t * (h + 1))
                lg = lg_ref[par, u, :, hs] + (madd + bias_ref[near, h])
                m_old = m_all[:, hs]
                m_new = jnp.maximum(m_old, jnp.max(lg, axis=0, keepdims=True))
                alpha = jnp.exp2(m_old - m_new)
                p = jnp.exp2(lg - m_new)
                ls.append(alpha * l_all[:, hs] + jnp.sum(p, axis=0, keepdims=True))
                ms.append(m_new)
                als.append(alpha)
                pt_ref[par, u, :, hs] = p.astype(bf16)
            al_ref[par, u] = jnp.concatenate(als, axis=1)
            m_all, l_all = jnp.concatenate(ms, axis=1), jnp.concatenate(ls, axis=1)
        return m_all, l_all

    def value_stage(j, par):
        for u in range(2):
            kb = jnp.maximum(2 * j + u, 0)
            pv = jnp.dot(avt_ref[kb], pt_ref[par, u], preferred_element_type=f32)
            acc_ref[...] = acc_ref[...] * al_ref[par, u] + pv

    def attn_trip(q, carry):
        for par in range(2):
            j = 2 * q + par
            value_stage(j - 1, 1 - par)
            carry = softmax_stage(j, par, *carry)
            logit_stage(j + 1, 1 - par)
        return carry

    nquad = (qi + 4) // 4
    logit_stage(0, 0)
    _, l_all = lax.fori_loop(0, nquad, attn_trip, (jnp.full((1, hw), NEG, f32), jnp.zeros((1, hw), f32)))
    value_stage(2 * nquad - 1, 1)

    out_t = acc_ref[...] / l_all
    out_t = jnp.concatenate([out_t[:, t * h:t * (h + 1)] for h in range(ATT_HEADS)], axis=0)
    o_ref[...] = out_t.T.astype(o_ref.dtype)


def _rel_bias_tiles(rel_bias):
    t = DSA_T
    ki = jnp.arange(t)[:, None]
    qj = jnp.arange(t)[None, :]
    far = rel_bias[REL_BUCKETS - 1].astype(f32)
    tiles = []
    for v in range(2):
        dist = jnp.maximum(v * t + qj - ki, 0)
        max_exact = REL_BUCKETS // 2
        nf = jnp.maximum(dist, 1).astype(f32)
        large = max_exact + (jnp.log(nf / max_exact) / math.log(REL_MAX_DIST / max_exact)
                             * (REL_BUCKETS - max_exact)).astype(i32)
        large = jnp.minimum(large, REL_BUCKETS - 1)
        bucket = jnp.where(dist < max_exact, dist, large)
        tiles.append(jnp.transpose((rel_bias[bucket].astype(f32) - far) * LOG2E, (2, 0, 1)))
    tiles.append(jnp.zeros_like(tiles[0]))
    return jnp.stack(tiles)


def _dsa(proj, iq, wt, ik, ak, avt, bias_tiles, batch, seq):
    n = proj.shape[0]
    t = DSA_T
    nq = seq // t
    assert nq % 4 == 0 and 2 * t >= REL_MAX_DIST
    ktop = min(TOPK_MAX, seq // 4)
    qrow = lambda width: pl.BlockSpec((t, width), lambda b, q: (b * nq + q, 0))
    kv = pl.BlockSpec((seq, 64), lambda b, q: (b, 0))
    hw = ATT_HEADS * t
    return pl.pallas_call(
        functools.partial(_dsa_kernel, ktop=ktop),
        out_shape=jax.ShapeDtypeStruct((n, ATT_W), bf16),
        grid=(batch, nq),
        in_specs=[qrow(256),
                  pl.BlockSpec((8, t), lambda b, q: (0, b * nq + q)),
                  pl.BlockSpec((t, COL_BLK), lambda b, q: (b * nq + q, CB_AQ)),
                  kv, kv,
                  pl.BlockSpec((nq, 64, t), lambda b, q: (b, 0, 0)),
                  pl.BlockSpec((3, ATT_HEADS, t, t), lambda b, q: (0, 0, 0, 0))],
        out_specs=qrow(ATT_W),
        scratch_shapes=[pltpu.VMEM((nq, t, t), f32),
                        pltpu.VMEM((IDX_HEADS * t, IDX_DH), bf16),
                        pltpu.VMEM((hw, ATT_DH), bf16),
                        pltpu.VMEM((2, 2, t, hw), f32),
                        pltpu.VMEM((2, 2, t, hw), bf16),
                        pltpu.VMEM((2, 2, 1, hw), f32),
                        pltpu.VMEM((ATT_DH, hw), f32),
                        pltpu.VMEM((nq, 1, t), f32)],
        compiler_params=_cparams(("parallel", "arbitrary")),
        name="dsa",
    )(iq, wt, proj, ik, ak, avt, bias_tiles)


def _merge_kernel(x_ref, ya_ref, yb_ref, ga_ref, gb_ref, wa_ref, wb_ref, wo_ref, g2_ref, wq_ref,
                  x1_ref, xn_ref, qp_ref):
    ha = jnp.dot(ya_ref[...], wa_ref[...], preferred_element_type=f32)
    hb = jnp.dot(yb_ref[...], wb_ref[...], preferred_element_type=f32)
    h = jax.nn.sigmoid(ga_ref[...]) * ha + jax.nn.sigmoid(gb_ref[...]) * hb
    x1 = x_ref[...] + jnp.dot(h.astype(bf16), wo_ref[...], preferred_element_type=f32)
    x1_ref[...] = x1
    ms = jnp.mean(x1 * x1, axis=-1, keepdims=True)
    xn = (x1 * lax.rsqrt(ms + EPS) * g2_ref[...]).astype(bf16)
    xn_ref[...] = xn
    qp_ref[...] = jnp.dot(xn, wq_ref[...], preferred_element_type=f32).astype(bf16)


def _merge(x2, ya, yb, proj, wa, wb, wo, g2, wq, tm=512):
    n = x2.shape[0]
    qw = wq.shape[1]
    row = lambda width: pl.BlockSpec((tm, width), lambda i: (i, 0))
    full = lambda a: pl.BlockSpec(a.shape, lambda i: (0, 0))
    return pl.pallas_call(
        _merge_kernel,
        out_shape=(jax.ShapeDtypeStruct((n, D_MODEL), f32),
                   jax.ShapeDtypeStruct((n, D_MODEL), bf16),
                   jax.ShapeDtypeStruct((n, qw), bf16)),
        grid=(n // tm,),
        in_specs=[row(D_MODEL), row(HG_W), row(ATT_W),
                  pl.BlockSpec((tm, D_MODEL), lambda i: (i, CB_GA)),
                  pl.BlockSpec((tm, D_MODEL), lambda i: (i, CB_GB)),
                  full(wa), full(wb), full(wo), full(g2), full(wq)],
        out_specs=(row(D_MODEL), row(D_MODEL), row(qw)),
        compiler_params=_cparams(("parallel",)),
        name="merge",
    )(x2, ya, yb, proj, proj, wa, wb, wo, g2, wq)


_PAIR_COUNTS = tuple(PEER_TOPK // (p + 1) for p in range(PEER_TOPK))
_NCAND = sum(_PAIR_COUNTS)
F32_SUBLANES = 8


def _exchange(v, i, j):
    v[i], v[j] = jnp.maximum(v[i], v[j]), jnp.minimum(v[i], v[j])


def _bitonic_sort(v):
    n = len(v)
    k = 2
    while k <= n:
        j = k // 2
        while j >= 1:
            for i in range(n):
                l = i ^ j
                if l > i:
                    if i & k:
                        _exchange(v, l, i)
                    else:
                        _exchange(v, i, l)
            j //= 2
        k *= 2


def _bitonic_merge(v):
    j = len(v) // 2
    while j >= 1:
        for i in range(len(v)):
            if i ^ j > i:
                _exchange(v, i, i ^ j)
        j //= 2


def _merge_sublanes(v, shifts):
    for shift in shifts:
        n = len(v)
        v = [jnp.maximum(v[g], pltpu.roll(v[n - 1 - g], shift, axis=0)) for g in range(n)]
        _bitonic_merge(v)
    return v


def _top_sorted(s):
    v = [s[F32_SUBLANES * g:F32_SUBLANES * (g + 1), :] for g in range(s.shape[0] // F32_SUBLANES)]
    _bitonic_sort(v)
    return _merge_sublanes(v[:PEER_TOPK], (4, 2, 1))


def _peer_route_kernel(qp_ref, k1_ref, k2_ref, n1_ref, a1_ref, r2_ref, e2_ref):
    tb = qp_ref.shape[0]
    sub = lax.broadcasted_iota(i32, (F32_SUBLANES, tb), 0)
    groups = PEER_NKEYS // F32_SUBLANES
    for h in range(PEER_HEADS):
        q1 = qp_ref[:, 2 * PEER_HALF * h:2 * PEER_HALF * h + PEER_HALF]
        q2 = qp_ref[:, 2 * PEER_HALF * h + PEER_HALF:2 * PEER_HALF * (h + 1)]
        s1 = lax.dot_general(k1_ref[h], q1, NT_DIMS, preferred_element_type=f32)
        s2 = lax.dot_general(k2_ref[h], q2, NT_DIMS, preferred_element_type=f32)
        a = _top_sorted(s1)
        b = _top_sorted(s2)
        nreg = pl.next_power_of_2(pl.cdiv(_NCAND, F32_SUBLANES))
        cand = [jnp.full((F32_SUBLANES, tb), -jnp.inf, f32) for _ in range(nreg)]
        r = 0
        for p in range(PEER_TOPK):
            for q in range(_PAIR_COUNTS[p]):
                cand[r // F32_SUBLANES] = jnp.where(sub == r % F32_SUBLANES, a[p] + b[q], cand[r // F32_SUBLANES])
                r += 1
        _bitonic_sort(cand)
        top = cand + [pltpu.roll(cand[len(cand) - 1 - g], 4, axis=0) for g in range(len(cand))]
        _bitonic_merge(top)
        top = _merge_sublanes(top, (2, 1))
        thr = top[PEER_TOPK - 1]
        mx = a[0] + b[0]
        z = jnp.zeros_like(mx)
        for tv in top:
            z = z + jnp.exp(tv - mx)
        s1g = s1.reshape(groups, F32_SUBLANES, tb)
        s2g = s2.reshape(groups, F32_SUBLANES, tb)
        n1 = jnp.zeros_like(s1g)
        r2 = jnp.zeros_like(s2g)
        for q in range(PEER_TOPK):
            n1 = n1 + jnp.where(s1g + b[q] >= thr, 1.0, 0.0)
            r2 = r2 + jnp.where(b[q] > s2g, 1.0, 0.0)
        n1_ref[h] = n1.reshape(PEER_NKEYS, tb)
        a1_ref[h] = (jnp.exp(s1g - a[0]) / z).reshape(PEER_NKEYS, tb)
        r2_ref[h] = r2.reshape(PEER_NKEYS, tb).astype(bf16)
        e2_ref[h] = jnp.exp(s2g - b[0]).reshape(PEER_NKEYS, tb).astype(bf16)


def _peer_route(qp, k1, k2, tb=512):
    n = qp.shape[0]
    big = lambda: pl.BlockSpec((PEER_HEADS, PEER_NKEYS, tb), lambda i: (0, 0, i))
    bigs = lambda dt: jax.ShapeDtypeStruct((PEER_HEADS, PEER_NKEYS, n), dt)
    return pl.pallas_call(
        _peer_route_kernel,
        out_shape=(bigs(f32), bigs(f32), bigs(bf16), bigs(bf16)),
        grid=(n // tb,),
        in_specs=[pl.BlockSpec((tb, qp.shape[1]), lambda i: (i, 0)),
                  pl.BlockSpec(k1.shape, lambda i: (0, 0, 0)),
                  pl.BlockSpec(k2.shape, lambda i: (0, 0, 0))],
        out_specs=(big(), big(), big(), big()),
        compiler_params=_cparams(("parallel",)),
        name="peer_route",
    )(qp, k1, k2)


PEER_EB = 1024


def _peer_dense_kernel(xn_ref, u_ref, vt_ref, n1_ref, a1_ref, r2_ref, e2_ref, x1_ref, gf_ref,
                       o_ref, acc_ref, act_ref, g_ref, *, final_norm):
    e = pl.program_id(1)
    tb = xn_ref.shape[0]

    @pl.when(e == 0)
    def _():
        acc_ref[...] = jnp.zeros_like(acc_ref)

    nblk = PEER_EB // PEER_NKEYS
    reps = PEER_NKEYS // BF16_SUBLANES

    def gates(blk, slot):
        for ii in range(nblk):
            i = jnp.minimum(blk * nblk + ii, PEER_NKEYS - 1)
            g = jnp.zeros((PEER_NKEYS, tb), bf16)
            for h in range(PEER_HEADS):
                n_row = jnp.broadcast_to(n1_ref[h, pl.ds(i, 1), :], (BF16_SUBLANES, tb)).astype(bf16)
                a_row = jnp.broadcast_to(a1_ref[h, pl.ds(i, 1), :], (BF16_SUBLANES, tb)).astype(bf16)
                n_all = jnp.tile(n_row, (reps, 1))
                a_all = jnp.tile(a_row, (reps, 1))
                g = g + jnp.where(r2_ref[h] < n_all, e2_ref[h] * a_all, jnp.zeros((), bf16))
            g_ref[slot, PEER_NKEYS * ii:PEER_NKEYS * (ii + 1), :] = g

    half = PEER_EB // 2

    def hidden(slot):
        base = PEER_EB * slot
        return [lax.dot_general(u_ref[base + half * k:base + half * (k + 1), :], xn_ref[...], NT_DIMS,
                                preferred_element_type=f32) for k in range(2)]

    def outputs(slot, hts):
        base = PEER_EB * slot
        for k in range(2):
            hh = hts[k]
            act = 0.5 * hh * (1.0 + lax.erf(hh * (2.0 ** -0.5)))
            rows = slice(half * k, half * (k + 1))
            act_ref[slot, rows, :] = (act * g_ref[slot, rows, :].astype(f32)).astype(bf16)
            acc_ref[...] += jnp.dot(vt_ref[:, base + half * k:base + half * (k + 1)], act_ref[slot, rows, :],
                                    preferred_element_type=f32)

    @pl.when(e == 0)
    def _():
        gates(1, 1)

    gates(2 * e, 0)
    hts0 = hidden(0)
    hts1 = hidden(1)
    outputs(0, hts0)
    outputs(1, hts1)
    gates(2 * e + 3, 1)

    @pl.when(e == pl.num_programs(1) - 1)
    def _():
        y = x1_ref[...] + acc_ref[...].T
        if final_norm:
            ms = jnp.mean(y * y, axis=-1, keepdims=True)
            y = y * lax.rsqrt(ms + EPS) * gf_ref[...]
        o_ref[...] = y


def _peer_dense(xn, u, vt, n1, a1, r2, e2, x1, gf, final_norm, tb=256):
    n = xn.shape[0]
    ne = u.shape[0]
    big = lambda: pl.BlockSpec((PEER_HEADS, PEER_NKEYS, tb), lambda t, e: (0, 0, t))
    return pl.pallas_call(
        functools.partial(_peer_dense_kernel, final_norm=final_norm),
        out_shape=jax.ShapeDtypeStruct((n, D_MODEL), f32),
        grid=(n // tb, ne // (2 * PEER_EB)),
        in_specs=[pl.BlockSpec((tb, D_MODEL), lambda t, e: (t, 0)),
                  pl.BlockSpec((2 * PEER_EB, D_MODEL), lambda t, e: (e, 0)),
                  pl.BlockSpec((D_MODEL, 2 * PEER_EB), lambda t, e: (0, e)),
                  big(), big(), big(), big(),
                  pl.BlockSpec((tb, D_MODEL), lambda t, e: (t, 0)),
                  pl.BlockSpec((1, D_MODEL), lambda t, e: (0, 0))],
        out_specs=pl.BlockSpec((tb, D_MODEL), lambda t, e: (t, 0)),
        scratch_shapes=[pltpu.VMEM((D_MODEL, tb), f32), pltpu.VMEM((2, PEER_EB, tb), bf16),
                        pltpu.VMEM((2, PEER_EB, tb), bf16)],
        compiler_params=_cparams(("parallel", "arbitrary")),
        name="peer_dense",
    )(xn, u, vt, n1, a1, r2, e2, x1, gf)


def _pack_w_in(w):
    offs = np.concatenate([[0], np.cumsum(np.array(IN_SIZES))]).tolist()
    seg = lambda k: w[:, offs[k]:offs[k + 1]]
    hq, hf, hi, hog, aq, ak, av, iq, ik, iw, ga, gb = [seg(k) for k in range(12)]
    pad = jnp.zeros((w.shape[0], 64 - IDX_HEADS), w.dtype)
    return jnp.concatenate([hq, hf, hi, hog, aq, iq, ak, av, ik, iw, pad, ga, gb], axis=1).astype(bf16)


def _layer(x2, batch, seq, norm_mix, w_in, lb, hg_norm, ikg, ikb, bias_tiles, w_up_a, w_up_b, w_out,
           norm_ffn, peer_wq, peer_keys, peer_u, peer_v):
    proj = _inproj(x2, norm_mix.reshape(1, -1), _pack_w_in(w_in))
    iq, ak, ik, avt, wt = _dsa_prep(proj, ikg.reshape(1, -1), ikb.reshape(1, -1))
    ya = _hgrn2(proj, lb.reshape(1, -1), hg_norm.reshape(1, -1), batch, seq)
    yb = _dsa(proj, iq, wt, ik, ak, avt, bias_tiles, batch, seq)
    x1, xn, qp = _merge(x2, ya, yb, proj, w_up_a.astype(bf16), w_up_b.astype(bf16), w_out.astype(bf16),
                        norm_ffn.reshape(1, -1), peer_wq.astype(bf16))
    route = _peer_route(qp, peer_keys[0].astype(bf16), peer_keys[1].astype(bf16))
    return xn, route, x1


def kernel(x, norm_mix, w_in, hg_lb, hg_norm, idx_k_norm_g, idx_k_norm_b, rel_bias, w_up_a, w_up_b, w_out,
           norm_ffn, peer_wq, peer_keys, peer_u, peer_v, norm_final):
    batch, seq, _ = x.shape
    depth = w_in.shape[0]
    lb_all = jnp.cumsum(jax.nn.softmax(hg_lb.astype(f32), axis=0), axis=0)
    bias_tiles = _rel_bias_tiles(rel_bias)
    x2 = x.reshape(batch * seq, D_MODEL)
    gf = norm_final.reshape(1, -1)
    for l in range(depth):
        xn, route, x1 = _layer(
            x2, batch, seq, norm_mix[l], w_in[l], lb_all[l], hg_norm[l], idx_k_norm_g[l], idx_k_norm_b[l],
            bias_tiles, w_up_a[l], w_up_b[l], w_out[l], norm_ffn[l], peer_wq[l], peer_keys[l], peer_u[l], peer_v[l])
        x2 = _peer_dense(xn, peer_u[l].astype(bf16), peer_v[l].astype(bf16).T, *route, x1, gf,
                         final_norm=(l == depth - 1))
    return x2.reshape(batch, seq, D_MODEL)
```

```python
import functools
import math

import jax
import jax.numpy as jnp
import numpy as np
from jax import lax
from jax.experimental import pallas as pl
from jax.experimental.pallas import tpu as pltpu

f32 = jnp.float32
bf16 = jnp.bfloat16
i32 = jnp.int32

D_MODEL = 1024
EPS = 1e-6
HG_HEADS, HG_D = 4, 128
HG_W = HG_HEADS * HG_D
ATT_HEADS, ATT_DH = 8, 64
ATT_W = ATT_HEADS * ATT_DH
IDX_HEADS, IDX_DH = 4, 64
TOPK_MAX = 256
REL_BUCKETS, REL_MAX_DIST = 32, 128
PEER_HEADS, PEER_NKEYS, PEER_HALF, PEER_TOPK = 8, 128, 128, 16
PEER_EXPERTS = PEER_NKEYS * PEER_NKEYS
IN_SIZES = (HG_W, HG_W, HG_W, HG_W, ATT_W, ATT_DH, ATT_DH, IDX_HEADS * IDX_DH, IDX_DH, IDX_HEADS, D_MODEL, D_MODEL)

LANES = 128
BF16_SUBLANES = 16
VMEM_LIMIT = 56 * 1024 * 1024

COL_BLK = 512
PROJ_W = 10 * COL_BLK
CB_HQ, CB_HF, CB_HI, CB_HOG, CB_AQ, CB_IDX = 0, 1, 2, 3, 4, 5
CB_GA, CB_GB = 3, 4
NEG = -1e30


def _cparams(sem):
    return pltpu.CompilerParams(dimension_semantics=sem, vmem_limit_bytes=VMEM_LIMIT)


def _inproj_kernel(x_ref, g_ref, w_ref, o_ref, xn_ref):
    @pl.when(pl.program_id(1) == 0)
    def _():
        x = x_ref[...]
        ms = jnp.mean(x * x, axis=-1, keepdims=True)
        xn_ref[...] = (x * lax.rsqrt(ms + EPS) * g_ref[...]).astype(bf16)

    o_ref[...] = jnp.dot(xn_ref[...], w_ref[...], preferred_element_type=f32)


def _inproj(x2, g, w, tm=1024):
    n = x2.shape[0]
    return pl.pallas_call(
        _inproj_kernel,
        out_shape=jax.ShapeDtypeStruct((n, PROJ_W), f32),
        grid=(n // tm, PROJ_W // COL_BLK),
        in_specs=[
            pl.BlockSpec((tm, D_MODEL), lambda i, j: (i, 0)),
            pl.BlockSpec((1, D_MODEL), lambda i, j: (0, 0)),
            pl.BlockSpec((D_MODEL, COL_BLK), lambda i, j: (0, j)),
        ],
        out_specs=pl.BlockSpec((tm, COL_BLK), lambda i, j: (i, j)),
        scratch_shapes=[pltpu.VMEM((tm, D_MODEL), bf16)],
        compiler_params=_cparams(("parallel", "arbitrary")),
        name="inproj",
    )(x2, g, w)


DSA_T = 128


def _dsa_prep_kernel(p_ref, g_ref, b_ref, iq_ref, ak_ref, ik_ref, avt_ref, wt_ref):
    t = DSA_T
    p = p_ref[...]
    iq_ref[...] = p[:, 0:256].astype(bf16)
    ak_ref[...] = p[:, 256:320].astype(bf16)
    ik = p[:, 384:448]
    mu = jnp.mean(ik, axis=-1, keepdims=True)
    var = jnp.mean(jnp.square(ik - mu), axis=-1, keepdims=True)
    y = (ik - mu) * lax.rsqrt(var + EPS) * g_ref[...] + b_ref[...]
    ik_ref[...] = y.astype(bf16)
    kv_t = p[:, 256:384].T
    for c in range(p.shape[0] // t):
        avt_ref[c] = kv_t[64:128, t * c:t * (c + 1)].astype(bf16)
    wt_ref[...] = p[:, 384:512].T[64:72, :] * (IDX_HEADS ** -0.5 * IDX_DH ** -0.5)


def _dsa_prep(proj, g, b, tm=1024):
    n = proj.shape[0]
    t = DSA_T
    row = lambda w: pl.BlockSpec((tm, w), lambda i: (i, 0))
    return pl.pallas_call(
        _dsa_prep_kernel,
        out_shape=(
            jax.ShapeDtypeStruct((n, 256), bf16),
            jax.ShapeDtypeStruct((n, 64), bf16),
            jax.ShapeDtypeStruct((n, 64), bf16),
            jax.ShapeDtypeStruct((n // t, 64, t), bf16),
            jax.ShapeDtypeStruct((8, n), f32),
        ),
        grid=(n // tm,),
        in_specs=[
            pl.BlockSpec((tm, COL_BLK), lambda i: (i, CB_IDX)),
            pl.BlockSpec((1, 64), lambda i: (0, 0)),
            pl.BlockSpec((1, 64), lambda i: (0, 0)),
        ],
        out_specs=(row(256), row(64), row(64),
                   pl.BlockSpec((tm // t, 64, t), lambda i: (i, 0, 0)),
                   pl.BlockSpec((8, tm), lambda i: (0, i))),
        compiler_params=_cparams(("parallel",)),
        name="dsa_prep",
    )(proj, g, b)


HG_CHUNK = 32


def _split3(a):
    a1 = a.astype(bf16)
    r1 = a - a1.astype(f32)
    a2 = r1.astype(bf16)
    r2 = r1 - a2.astype(f32)
    return a1, a2, r2.astype(bf16)


def _hgrn2_kernel(q_ref, f_ref, i_ref, og_ref, lb_ref, gn_ref, o_ref, st_ref, oacc_ref, *, ts):
    c = HG_CHUNK

    @pl.when(pl.program_id(1) == 0)
    def _():
        st_ref[...] = jnp.zeros_like(st_ref)

    r_io = lax.broadcasted_iota(i32, (c, c), 0)
    c_io = lax.broadcasted_iota(i32, (c, c), 1)
    causal = c_io <= r_io
    tri = causal.astype(bf16)
    mid = c // 2 - 1

    heads = range(HG_HEADS)
    lanes = [slice(HG_D * h, HG_D * (h + 1)) for h in heads]

    def chunk_pair(cp, carry):
        chains = [(pl.ds(pl.multiple_of((2 * cp + u) * c, c), c), h) for u in range(2) for h in heads]
        q, k, v, b = [], [], [], []
        for rows, h in chains:
            qr = q_ref[rows, lanes[h]]
            q.append(qr * jax.nn.sigmoid(qr))
            lb = lb_ref[:, lanes[h]]
            fg = lb + (1.0 - lb) * jax.nn.sigmoid(f_ref[rows, lanes[h]])
            k.append(1.0 - fg)
            v.append(i_ref[rows, lanes[h]].astype(bf16))
            l1, l2, l3 = _split3(jnp.log(fg))
            b.append(jnp.dot(tri, l1, preferred_element_type=f32)
                     + jnp.dot(tri, l2, preferred_element_type=f32)
                     + jnp.dot(tri, l3, preferred_element_type=f32))
        a = []
        for n in range(len(chains)):
            b_mid = b[n][mid:mid + 1, :]
            qt = (q[n] * jnp.exp(b[n] - b_mid)).astype(bf16)
            kt = (k[n] * jnp.exp(b_mid - b[n])).astype(bf16)
            a.append(lax.dot_general(qt, kt, NT_DIMS, preferred_element_type=f32))
        intra = [jnp.dot(jnp.where(causal, a[n], 0.0).astype(bf16), v[n], preferred_element_type=f32)
                 for n in range(len(chains))]
        for n, (rows, h) in enumerate(chains):
            b_last = b[n][c - 1:c, :]
            st = st_ref[h]
            qe = (q[n] * jnp.exp(b[n])).astype(bf16)
            inter = lax.dot_general(qe, st.astype(bf16), NT_DIMS, preferred_element_type=f32)
            oacc_ref[rows, lanes[h]] = inter + intra[n]
            ke = (k[n] * jnp.exp(b_last - b[n])).astype(bf16)
            upd = lax.dot_general(v[n], ke, (((0,), (0,)), ((), ())), preferred_element_type=f32)
            st_ref[h] = st * jnp.exp(b_last) + upd
        return carry

    lax.fori_loop(0, ts // (2 * c), chunk_pair, 0)

    for h in range(HG_HEADS):
        sl = slice(HG_D * h, HG_D * (h + 1))
        o = oacc_ref[:, sl]
        ms = jnp.mean(o * o, axis=-1, keepdims=True)
        og = og_ref[:, sl]
        y = o * lax.rsqrt(ms + EPS) * gn_ref[:, sl] * (og * jax.nn.sigmoid(og))
        o_ref[:, sl] = y.astype(o_ref.dtype)


def _hgrn2(proj, lb, gn, batch, seq, ts=256):
    n = proj.shape[0]
    nsb = seq // ts
    col = lambda cb: pl.BlockSpec((ts, COL_BLK), lambda b, s, cb=cb: (b * nsb + s, cb))
    return pl.pallas_call(
        functools.partial(_hgrn2_kernel, ts=ts),
        out_shape=jax.ShapeDtypeStruct((n, HG_W), bf16),
        grid=(batch, nsb),
        in_specs=[col(CB_HQ), col(CB_HF), col(CB_HI), col(CB_HOG),
                  pl.BlockSpec((1, HG_W), lambda b, s: (0, 0)),
                  pl.BlockSpec((1, HG_W), lambda b, s: (0, 0))],
        out_specs=pl.BlockSpec((ts, HG_W), lambda b, s: (b * nsb + s, 0)),
        scratch_shapes=[pltpu.VMEM((HG_HEADS, HG_D, HG_D), f32), pltpu.VMEM((ts, HG_W), f32)],
        compiler_params=_cparams(("parallel", "arbitrary")),
        name="hgrn2",
    )(proj, proj, proj, proj, lb, gn)


LOG2E = 1.4426950408889634
FLT_MAX = 3.4028234663852886e38
NT_DIMS = (((1,), (1,)), ((), ()))


def _key_to_float(key):
    bits = jnp.where(key >= 0, key, key ^ jnp.int32(0x7FFFFFFF))
    return pltpu.bitcast(bits, f32)


def _dsa_kernel(iq_ref, wt_ref, aq_ref, ik_ref, ak_ref, avt_ref, bias_ref, o_ref,
                sc_ref, iqs_ref, qs_ref, lg_ref, pt_ref, al_ref, acc_ref, bef_ref, *, ktop):
    t = DSA_T
    qi = pl.program_id(1)
    npair = (qi + 2) // 2
    krow = lax.broadcasted_iota(i32, (t, t), 0)
    qcol = lax.broadcasted_iota(i32, (t, t), 1)
    qpos = qi * t + lax.broadcasted_iota(i32, (1, t), 1)

    iq = iq_ref[...]
    for j in range(IDX_HEADS):
        iqs_ref[t * j:t * (j + 1), :] = iq[:, IDX_DH * j:IDX_DH * (j + 1)]
    aq = aq_ref[...] * (ATT_DH ** -0.5 * LOG2E)
    for h in range(ATT_HEADS):
        qs_ref[t * h:t * (h + 1), :] = aq[:, ATT_DH * h:ATT_DH * (h + 1)].astype(bf16)
    wt = wt_ref[...]

    def score_pair(kp):
        ksl = pl.ds(pl.multiple_of(kp * 2 * t, 2 * t), 2 * t)
        rel = lax.dot_general(ik_ref[ksl, :], iqs_ref[...], NT_DIMS, preferred_element_type=f32)
        for u in range(2):
            kb = 2 * kp + u
            s = jnp.zeros((t, t), f32)
            for j in range(IDX_HEADS):
                s = s + wt[j:j + 1, :] * jnp.maximum(rel[t * u:t * (u + 1), t * j:t * (j + 1)], 0.0)
            sc_ref[kb] = jnp.where((kb * t + krow) <= (qi * t + qcol), s, -jnp.inf)

    def score_quad(kq, carry):
        score_pair(2 * kq)
        score_pair(2 * kq + 1)
        return carry

    lax.fori_loop(0, (qi + 4) // 4, score_quad, 0)

    def count_keys(pred):
        def body(kp, acc):
            acc = acc + pred(sc_ref[2 * kp], 2 * kp).astype(i32)
            return acc + pred(sc_ref[2 * kp + 1], 2 * kp + 1).astype(i32)
        acc = lax.fori_loop(0, npair, body, jnp.zeros((t, t), i32))
        return jnp.sum(acc, axis=0, keepdims=True)

    take_all = qpos < ktop
    few_pos = count_keys(lambda s, kb: s > 0.0) < ktop

    def unsettled(key, cnt):
        done = take_all | (cnt == ktop) | (few_pos & (key == 0))
        return jnp.min(done.astype(i32)) == 0

    bits_per_check = 4

    def bit_cond(c):
        return (c[0] < 32 // bits_per_check) & unsettled(c[1], c[2])

    def bit_step(c):
        g, key, cnt = c
        for b in range(bits_per_check):
            cand = key + jnp.left_shift(jnp.int32(1), 31 - b - bits_per_check * g)
            cf = _key_to_float(cand)
            cc = count_keys(lambda s, kb: s >= cf)
            ok = cc >= ktop
            key = jnp.where(ok, cand, key)
            cnt = jnp.where(ok, cc, cnt)
        return g + 1, key, cnt

    _, key, cnt = lax.while_loop(bit_cond, bit_step,
                                 (jnp.int32(0), jnp.full((1, t), -2 ** 31, i32), qpos + 1))
    thr = jnp.where(take_all, -FLT_MAX, _key_to_float(key))

    @pl.when(jnp.max(jnp.where(take_all, ktop, cnt)) > ktop)
    def _():
        def tally(kb, carry):
            gt, before = carry
            s = sc_ref[kb]
            bef_ref[kb] = before
            ties = jnp.sum(jnp.where(s == thr, 1.0, 0.0), axis=0, keepdims=True)
            return gt + jnp.where(s > thr, 1.0, 0.0), before + ties

        zero = jnp.zeros((1, t), f32)
        nquad = (qi + 4) // 4
        gt, _ = lax.fori_loop(0, 4 * nquad, tally, (jnp.zeros((t, t), f32), zero))
        need = ktop - jnp.sum(gt, axis=0, keepdims=True)
        tri = (qcol <= krow).astype(bf16)

        def drop(kb):
            s = sc_ref[kb]
            tie = s == thr
            run = bef_ref[kb] + jnp.dot(tri, jnp.where(tie, 1.0, 0.0).astype(bf16), preferred_element_type=f32)
            sc_ref[kb] = jnp.where(tie & (run > need), -jnp.inf, s)

        def drop_quad(kq, carry):
            for u in range(4):
                drop(4 * kq + u)
            return carry

        lax.fori_loop(0, nquad, drop_quad, 0)

    nq = sc_ref.shape[0]
    hw = ATT_HEADS * t
    acc_ref[...] = jnp.zeros(acc_ref.shape, f32)
    pt_ref[1] = jnp.zeros(pt_ref.shape[1:], bf16)
    al_ref[1] = jnp.ones(al_ref.shape[1:], f32)

    def logit_stage(j, par):
        for u in range(2):
            kb = jnp.minimum(2 * j + u, nq - 1)
            kblk = ak_ref[pl.ds(pl.multiple_of(kb * t, t), t), :]
            lg_ref[par, u] = lax.dot_general(kblk, qs_ref[...], NT_DIMS, preferred_element_type=f32)

    def softmax_stage(j, par, m_all, l_all):
        for u in range(2):
            kb = 2 * j + u
            madd = jnp.where(sc_ref[kb] >= thr, 0.0, NEG)
            near = jnp.clip(qi - kb, 0, 2)
            ms, ls, als = [], [], []
            for h in range(ATT_HEADS):
                hs = slice(t * h, t * (h + 1))
                lg = lg_ref[par, u, :, hs] + (madd + bias_ref[near, h])
                m_old = m_all[:, hs]
                m_new = jnp.maximum(m_old, jnp.max(lg, axis=0, keepdims=True))
                alpha = jnp.exp2(m_old - m_new)
                p = jnp.exp2(lg - m_new)
                ls.append(alpha * l_all[:, hs] + jnp.sum(p, axis=0, keepdims=True))
                ms.append(m_new)
                als.append(alpha)
                pt_ref[par, u, :, hs] = p.astype(bf16)
            al_ref[par, u] = jnp.concatenate(als, axis=1)
            m_all, l_all = jnp.concatenate(ms, axis=1), jnp.concatenate(ls, axis=1)
        return m_all, l_all

    def value_stage(j, par):
        for u in range(2):
            kb = jnp.maximum(2 * j + u, 0)
            pv = jnp.dot(avt_ref[kb], pt_ref[par, u], preferred_element_type=f32)
            acc_ref[...] = acc_ref[...] * al_ref[par, u] + pv

    def attn_trip(q, carry):
        for par in range(2):
            j = 2 * q + par
            value_stage(j - 1, 1 - par)
            carry = softmax_stage(j, par, *carry)
            logit_stage(j + 1, 1 - par)
        return carry

    nquad = (qi + 4) // 4
    logit_stage(0, 0)
    _, l_all = lax.fori_loop(0, nquad, attn_trip, (jnp.full((1, hw), NEG, f32), jnp.zeros((1, hw), f32)))
    value_stage(2 * nquad - 1, 1)

    out_t = acc_ref[...] / l_all
    out_t = jnp.concatenate([out_t[:, t * h:t * (h + 1)] for h in range(ATT_HEADS)], axis=0)
    o_ref[...] = out_t.T.astype(o_ref.dtype)


def _rel_bias_tiles(rel_bias):
    t = DSA_T
    ki = jnp.arange(t)[:, None]
    qj = jnp.arange(t)[None, :]
    far = rel_bias[REL_BUCKETS - 1].astype(f32)
    tiles = []
    for v in range(2):
        dist = jnp.maximum(v * t + qj - ki, 0)
        max_exact = REL_BUCKETS // 2
        nf = jnp.maximum(dist, 1).astype(f32)
        large = max_exact + (jnp.log(nf / max_exact) / math.log(REL_MAX_DIST / max_exact)
                             * (REL_BUCKETS - max_exact)).astype(i32)
        large = jnp.minimum(large, REL_BUCKETS - 1)
        bucket = jnp.where(dist < max_exact, dist, large)
        tiles.append(jnp.transpose((rel_bias[bucket].astype(f32) - far) * LOG2E, (2, 0, 1)))
    tiles.append(jnp.zeros_like(tiles[0]))
    return jnp.stack(tiles)


def _dsa(proj, iq, wt, ik, ak, avt, bias_tiles, batch, seq):
    n = proj.shape[0]
    t = DSA_T
    nq = seq // t
    assert nq % 4 == 0 and 2 * t >= REL_MAX_DIST
    ktop = min(TOPK_MAX, seq // 4)
    qrow = lambda width: pl.BlockSpec((t, width), lambda b, q: (b * nq + q, 0))
    kv = pl.BlockSpec((seq, 64), lambda b, q: (b, 0))
    hw = ATT_HEADS * t
    return pl.pallas_call(
        functools.partial(_dsa_kernel, ktop=ktop),
        out_shape=jax.ShapeDtypeStruct((n, ATT_W), bf16),
        grid=(batch, nq),
        in_specs=[qrow(256),
                  pl.BlockSpec((8, t), lambda b, q: (0, b * nq + q)),
                  pl.BlockSpec((t, COL_BLK), lambda b, q: (b * nq + q, CB_AQ)),
                  kv, kv,
                  pl.BlockSpec((nq, 64, t), lambda b, q: (b, 0, 0)),
                  pl.BlockSpec((3, ATT_HEADS, t, t), lambda b, q: (0, 0, 0, 0))],
        out_specs=qrow(ATT_W),
        scratch_shapes=[pltpu.VMEM((nq, t, t), f32),
                        pltpu.VMEM((IDX_HEADS * t, IDX_DH), bf16),
                        pltpu.VMEM((hw, ATT_DH), bf16),
                        pltpu.VMEM((2, 2, t, hw), f32),
                        pltpu.VMEM((2, 2, t, hw), bf16),
                        pltpu.VMEM((2, 2, 1, hw), f32),
                        pltpu.VMEM((ATT_DH, hw), f32),
                        pltpu.VMEM((nq, 1, t), f32)],
        compiler_params=_cparams(("parallel", "arbitrary")),
        name="dsa",
    )(iq, wt, proj, ik, ak, avt, bias_tiles)


def _merge_kernel(x_ref, ya_ref, yb_ref, ga_ref, gb_ref, wa_ref, wb_ref, wo_ref, g2_ref, wq_ref,
                  x1_ref, xnt_ref, qp_ref):
    ha = jnp.dot(ya_ref[...], wa_ref[...], preferred_element_type=f32)
    hb = jnp.dot(yb_ref[...], wb_ref[...], preferred_element_type=f32)
    h = jax.nn.sigmoid(ga_ref[...]) * ha + jax.nn.sigmoid(gb_ref[...]) * hb
    x1 = x_ref[...] + jnp.dot(h.astype(bf16), wo_ref[...], preferred_element_type=f32)
    x1_ref[...] = x1
    ms = jnp.mean(x1 * x1, axis=-1, keepdims=True)
    xn = x1 * lax.rsqrt(ms + EPS) * g2_ref[...]
    xnt_ref[...] = xn.T.astype(bf16)
    qp_ref[...] = jnp.dot(xn.astype(bf16), wq_ref[...], preferred_element_type=f32).astype(bf16)


def _merge(x2, ya, yb, proj, wa, wb, wo, g2, wq, tm=512):
    n = x2.shape[0]
    qw = wq.shape[1]
    row = lambda width: pl.BlockSpec((tm, width), lambda i: (i, 0))
    full = lambda a: pl.BlockSpec(a.shape, lambda i: (0, 0))
    return pl.pallas_call(
        _merge_kernel,
        out_shape=(jax.ShapeDtypeStruct((n, D_MODEL), f32),
                   jax.ShapeDtypeStruct((D_MODEL, n), bf16),
                   jax.ShapeDtypeStruct((n, qw), bf16)),
        grid=(n // tm,),
        in_specs=[row(D_MODEL), row(HG_W), row(ATT_W),
                  pl.BlockSpec((tm, D_MODEL), lambda i: (i, CB_GA)),
                  pl.BlockSpec((tm, D_MODEL), lambda i: (i, CB_GB)),
                  full(wa), full(wb), full(wo), full(g2), full(wq)],
        out_specs=(row(D_MODEL), pl.BlockSpec((D_MODEL, tm), lambda i: (0, i)), row(qw)),
        compiler_params=_cparams(("parallel",)),
        name="merge",
    )(x2, ya, yb, proj, proj, wa, wb, wo, g2, wq)


_PAIR_COUNTS = tuple(PEER_TOPK // (p + 1) for p in range(PEER_TOPK))
_NCAND = sum(_PAIR_COUNTS)
F32_SUBLANES = 8


def _exchange(v, i, j):
    v[i], v[j] = jnp.maximum(v[i], v[j]), jnp.minimum(v[i], v[j])


def _bitonic_sort(v):
    n = len(v)
    k = 2
    while k <= n:
        j = k // 2
        while j >= 1:
            for i in range(n):
                l = i ^ j
                if l > i:
                    if i & k:
                        _exchange(v, l, i)
                    else:
                        _exchange(v, i, l)
            j //= 2
        k *= 2


def _bitonic_merge(v):
    j = len(v) // 2
    while j >= 1:
        for i in range(len(v)):
            if i ^ j > i:
                _exchange(v, i, i ^ j)
        j //= 2


def _merge_sublanes(v, shifts):
    for shift in shifts:
        n = len(v)
        v = [jnp.maximum(v[g], pltpu.roll(v[n - 1 - g], shift, axis=0)) for g in range(n)]
        _bitonic_merge(v)
    return v


def _top_sorted(s):
    v = [s[F32_SUBLANES * g:F32_SUBLANES * (g + 1), :] for g in range(s.shape[0] // F32_SUBLANES)]
    _bitonic_sort(v)
    return _merge_sublanes(v[:PEER_TOPK], (4, 2, 1))


def _peer_route_kernel(qp_ref, k1_ref, k2_ref, n1_ref, a1_ref, r2_ref, e2_ref):
    tb = qp_ref.shape[0]
    sub = lax.broadcasted_iota(i32, (F32_SUBLANES, tb), 0)
    groups = PEER_NKEYS // F32_SUBLANES
    for h in range(PEER_HEADS):
        q1 = qp_ref[:, 2 * PEER_HALF * h:2 * PEER_HALF * h + PEER_HALF]
        q2 = qp_ref[:, 2 * PEER_HALF * h + PEER_HALF:2 * PEER_HALF * (h + 1)]
        s1 = lax.dot_general(k1_ref[h], q1, NT_DIMS, preferred_element_type=f32)
        s2 = lax.dot_general(k2_ref[h], q2, NT_DIMS, preferred_element_type=f32)
        a = _top_sorted(s1)
        b = _top_sorted(s2)
        nreg = pl.next_power_of_2(pl.cdiv(_NCAND, F32_SUBLANES))
        cand = [jnp.full((F32_SUBLANES, tb), -jnp.inf, f32) for _ in range(nreg)]
        r = 0
        for p in range(PEER_TOPK):
            for q in range(_PAIR_COUNTS[p]):
                cand[r // F32_SUBLANES] = jnp.where(sub == r % F32_SUBLANES, a[p] + b[q], cand[r // F32_SUBLANES])
                r += 1
        _bitonic_sort(cand)
        top = cand + [pltpu.roll(cand[len(cand) - 1 - g], 4, axis=0) for g in range(len(cand))]
        _bitonic_merge(top)
        top = _merge_sublanes(top, (2, 1))
        thr = top[PEER_TOPK - 1]
        mx = a[0] + b[0]
        z = jnp.zeros_like(mx)
        for tv in top:
            z = z + jnp.exp(tv - mx)
        s1g = s1.reshape(groups, F32_SUBLANES, tb)
        s2g = s2.reshape(groups, F32_SUBLANES, tb)
        n1 = jnp.zeros_like(s1g)
        r2 = jnp.zeros_like(s2g)
        for q in range(PEER_TOPK):
            n1 = n1 + jnp.where(s1g + b[q] >= thr, 1.0, 0.0)
            r2 = r2 + jnp.where(b[q] > s2g, 1.0, 0.0)
        n1_ref[h] = n1.reshape(PEER_NKEYS, tb)
        a1_ref[h] = (jnp.exp(s1g - a[0]) / z).reshape(PEER_NKEYS, tb)
        r2_ref[h] = r2.reshape(PEER_NKEYS, tb).astype(bf16)
        e2_ref[h] = jnp.exp(s2g - b[0]).reshape(PEER_NKEYS, tb).astype(bf16)


def _peer_route(qp, k1, k2, tb=512):
    n = qp.shape[0]
    big = lambda: pl.BlockSpec((PEER_HEADS, PEER_NKEYS, tb), lambda i: (0, 0, i))
    bigs = lambda dt: jax.ShapeDtypeStruct((PEER_HEADS, PEER_NKEYS, n), dt)
    return pl.pallas_call(
        _peer_route_kernel,
        out_shape=(bigs(f32), bigs(f32), bigs(bf16), bigs(bf16)),
        grid=(n // tb,),
        in_specs=[pl.BlockSpec((tb, qp.shape[1]), lambda i: (i, 0)),
                  pl.BlockSpec(k1.shape, lambda i: (0, 0, 0)),
                  pl.BlockSpec(k2.shape, lambda i: (0, 0, 0))],
        out_specs=(big(), big(), big(), big()),
        compiler_params=_cparams(("parallel",)),
        name="peer_route",
    )(qp, k1, k2)


PEER_EB = 512
PEER_NB = 2


def _peer_dense_kernel(xt_ref, u_ref, vt_ref, n1_ref, a1_ref, r2_ref, e2_ref, x1_ref, gf_ref,
                       o_ref, acc_ref, act_ref, g_ref, *, final_norm):
    e = pl.program_id(1)
    tb = xt_ref.shape[1]

    @pl.when(e == 0)
    def _():
        acc_ref[...] = jnp.zeros_like(acc_ref)

    nblk = PEER_EB // PEER_NKEYS
    reps = PEER_NKEYS // BF16_SUBLANES

    def gates(blk, slot):
        for ii in range(nblk):
            i = jnp.minimum(blk * nblk + ii, PEER_NKEYS - 1)
            g = jnp.zeros((PEER_NKEYS, tb), bf16)
            for h in range(PEER_HEADS):
                n_row = jnp.broadcast_to(n1_ref[h, pl.ds(i, 1), :], (BF16_SUBLANES, tb)).astype(bf16)
                a_row = jnp.broadcast_to(a1_ref[h, pl.ds(i, 1), :], (BF16_SUBLANES, tb)).astype(bf16)
                n_all = jnp.tile(n_row, (reps, 1))
                a_all = jnp.tile(a_row, (reps, 1))
                g = g + jnp.where(r2_ref[h] < n_all, e2_ref[h] * a_all, jnp.zeros((), bf16))
            g_ref[slot, PEER_NKEYS * ii:PEER_NKEYS * (ii + 1), :] = g

    def hidden(slot):
        return jnp.dot(u_ref[PEER_EB * slot:PEER_EB * (slot + 1), :], xt_ref[...], preferred_element_type=f32)

    def outputs(slot, hh):
        act = 0.5 * hh * (1.0 + lax.erf(hh * (2.0 ** -0.5)))
        act_ref[slot] = (act * g_ref[slot].astype(f32)).astype(bf16)
        acc_ref[...] += jnp.dot(vt_ref[:, PEER_EB * slot:PEER_EB * (slot + 1)], act_ref[slot],
                                preferred_element_type=f32)

    @pl.when(e == 0)
    def _():
        for slot in range(1, PEER_NB):
            gates(slot, slot)

    gates(PEER_NB * e, 0)
    hts = [hidden(slot) for slot in range(PEER_NB)]
    for slot in range(PEER_NB):
        outputs(slot, hts[slot])
    for slot in range(1, PEER_NB):
        gates(PEER_NB * (e + 1) + slot, slot)

    @pl.when(e == pl.num_programs(1) - 1)
    def _():
        y = x1_ref[...] + acc_ref[...].T
        if final_norm:
            ms = jnp.mean(y * y, axis=-1, keepdims=True)
            y = y * lax.rsqrt(ms + EPS) * gf_ref[...]
        o_ref[...] = y


def _peer_dense(xt, u, vt, n1, a1, r2, e2, x1, gf, final_norm, tb=512):
    n = xt.shape[1]
    ne = u.shape[0]
    step = PEER_NB * PEER_EB
    big = lambda: pl.BlockSpec((PEER_HEADS, PEER_NKEYS, tb), lambda t, e: (0, 0, t))
    return pl.pallas_call(
        functools.partial(_peer_dense_kernel, final_norm=final_norm),
        out_shape=jax.ShapeDtypeStruct((n, D_MODEL), f32),
        grid=(n // tb, ne // step),
        in_specs=[pl.BlockSpec((D_MODEL, tb), lambda t, e: (0, t)),
                  pl.BlockSpec((step, D_MODEL), lambda t, e: (e, 0)),
                  pl.BlockSpec((D_MODEL, step), lambda t, e: (0, e)),
                  big(), big(), big(), big(),
                  pl.BlockSpec((tb, D_MODEL), lambda t, e: (t, 0)),
                  pl.BlockSpec((1, D_MODEL), lambda t, e: (0, 0))],
        out_specs=pl.BlockSpec((tb, D_MODEL), lambda t, e: (t, 0)),
        scratch_shapes=[pltpu.VMEM((D_MODEL, tb), f32), pltpu.VMEM((PEER_NB, PEER_EB, tb), bf16),
                        pltpu.VMEM((PEER_NB, PEER_EB, tb), bf16)],
        compiler_params=_cparams(("parallel", "arbitrary")),
        name="peer_dense",
    )(xt, u, vt, n1, a1, r2, e2, x1, gf)


def _pack_w_in(w):
    offs = np.concatenate([[0], np.cumsum(np.array(IN_SIZES))]).tolist()
    seg = lambda k: w[:, offs[k]:offs[k + 1]]
    hq, hf, hi, hog, aq, ak, av, iq, ik, iw, ga, gb = [seg(k) for k in range(12)]
    pad = jnp.zeros((w.shape[0], 64 - IDX_HEADS), w.dtype)
    return jnp.concatenate([hq, hf, hi, hog, aq, iq, ak, av, ik, iw, pad, ga, gb], axis=1).astype(bf16)


def _layer(x2, batch, seq, norm_mix, w_in, lb, hg_norm, ikg, ikb, bias_tiles, w_up_a, w_up_b, w_out,
           norm_ffn, peer_wq, peer_keys, peer_u, peer_v):
    proj = _inproj(x2, norm_mix.reshape(1, -1), _pack_w_in(w_in))
    iq, ak, ik, avt, wt = _dsa_prep(proj, ikg.reshape(1, -1), ikb.reshape(1, -1))
    ya = _hgrn2(proj, lb.reshape(1, -1), hg_norm.reshape(1, -1), batch, seq)
    yb = _dsa(proj, iq, wt, ik, ak, avt, bias_tiles, batch, seq)
    x1, xn, qp = _merge(x2, ya, yb, proj, w_up_a.astype(bf16), w_up_b.astype(bf16), w_out.astype(bf16),
                        norm_ffn.reshape(1, -1), peer_wq.astype(bf16))
    route = _peer_route(qp, peer_keys[0].astype(bf16), peer_keys[1].astype(bf16))
    return xn, route, x1


def kernel(x, norm_mix, w_in, hg_lb, hg_norm, idx_k_norm_g, idx_k_norm_b, rel_bias, w_up_a, w_up_b, w_out,
           norm_ffn, peer_wq, peer_keys, peer_u, peer_v, norm_final):
    batch, seq, _ = x.shape
    depth = w_in.shape[0]
    lb_all = jnp.cumsum(jax.nn.softmax(hg_lb.astype(f32), axis=0), axis=0)
    bias_tiles = _rel_bias_tiles(rel_bias)
    x2 = x.reshape(batch * seq, D_MODEL)
    gf = norm_final.reshape(1, -1)
    for l in range(depth):
        xn, route, x1 = _layer(
            x2, batch, seq, norm_mix[l], w_in[l], lb_all[l], hg_norm[l], idx_k_norm_g[l], idx_k_norm_b[l],
            bias_tiles, w_up_a[l], w_up_b[l], w_out[l], norm_ffn[l], peer_wq[l], peer_keys[l], peer_u[l], peer_v[l])
        x2 = _peer_dense(xn, peer_u[l].astype(bf16), peer_v[l].astype(bf16).T, *route, x1, gf,
                         final_norm=(l == depth - 1))
    return x2.reshape(batch, seq, D_MODEL)
```

```python
import functools
import math

import jax
import jax.numpy as jnp
import numpy as np
from jax import lax
from jax.experimental import pallas as pl
from jax.experimental.pallas import tpu as pltpu

f32 = jnp.float32
bf16 = jnp.bfloat16
i32 = jnp.int32

D_MODEL = 1024
EPS = 1e-6
HG_HEADS, HG_D = 4, 128
HG_W = HG_HEADS * HG_D
ATT_HEADS, ATT_DH = 8, 64
ATT_W = ATT_HEADS * ATT_DH
IDX_HEADS, IDX_DH = 4, 64
TOPK_MAX = 256
REL_BUCKETS, REL_MAX_DIST = 32, 128
PEER_HEADS, PEER_NKEYS, PEER_HALF, PEER_TOPK = 8, 128, 128, 16
PEER_EXPERTS = PEER_NKEYS * PEER_NKEYS
IN_SIZES = (HG_W, HG_W, HG_W, HG_W, ATT_W, ATT_DH, ATT_DH, IDX_HEADS * IDX_DH, IDX_DH, IDX_HEADS, D_MODEL, D_MODEL)

LANES = 128
BF16_SUBLANES = 16
VMEM_LIMIT = 56 * 1024 * 1024

COL_BLK = 512
PROJ_W = 10 * COL_BLK
CB_HQ, CB_HF, CB_HI, CB_HOG, CB_AQ, CB_IDX = 0, 1, 2, 3, 4, 5
CB_GA, CB_GB = 3, 4
NEG = -1e30


def _cparams(sem):
    return pltpu.CompilerParams(dimension_semantics=sem, vmem_limit_bytes=VMEM_LIMIT)


def _inproj_kernel(x_ref, g_ref, w_ref, o_ref, xn_ref):
    @pl.when(pl.program_id(1) == 0)
    def _():
        x = x_ref[...]
        ms = jnp.mean(x * x, axis=-1, keepdims=True)
        xn_ref[...] = (x * lax.rsqrt(ms + EPS) * g_ref[...]).astype(bf16)

    o_ref[...] = jnp.dot(xn_ref[...], w_ref[...], preferred_element_type=f32)


def _inproj(x2, g, w, tm=1024):
    n = x2.shape[0]
    return pl.pallas_call(
        _inproj_kernel,
        out_shape=jax.ShapeDtypeStruct((n, PROJ_W), f32),
        grid=(n // tm, PROJ_W // COL_BLK),
        in_specs=[
            pl.BlockSpec((tm, D_MODEL), lambda i, j: (i, 0)),
            pl.BlockSpec((1, D_MODEL), lambda i, j: (0, 0)),
            pl.BlockSpec((D_MODEL, COL_BLK), lambda i, j: (0, j)),
        ],
        out_specs=pl.BlockSpec((tm, COL_BLK), lambda i, j: (i, j)),
        scratch_shapes=[pltpu.VMEM((tm, D_MODEL), bf16)],
        compiler_params=_cparams(("parallel", "arbitrary")),
        name="inproj",
    )(x2, g, w)


DSA_T = 128


def _dsa_prep_kernel(p_ref, g_ref, b_ref, iq_ref, ak_ref, ik_ref, avt_ref, wt_ref):
    t = DSA_T
    p = p_ref[...]
    iq_ref[...] = p[:, 0:256].astype(bf16)
    ak_ref[...] = p[:, 256:320].astype(bf16)
    ik = p[:, 384:448]
    mu = jnp.mean(ik, axis=-1, keepdims=True)
    var = jnp.mean(jnp.square(ik - mu), axis=-1, keepdims=True)
    y = (ik - mu) * lax.rsqrt(var + EPS) * g_ref[...] + b_ref[...]
    ik_ref[...] = y.astype(bf16)
    kv_t = p[:, 256:384].T
    for c in range(p.shape[0] // t):
        avt_ref[c] = kv_t[64:128, t * c:t * (c + 1)].astype(bf16)
    wt_ref[...] = p[:, 384:512].T[64:72, :] * (IDX_HEADS ** -0.5 * IDX_DH ** -0.5)


def _dsa_prep(proj, g, b, tm=1024):
    n = proj.shape[0]
    t = DSA_T
    row = lambda w: pl.BlockSpec((tm, w), lambda i: (i, 0))
    return pl.pallas_call(
        _dsa_prep_kernel,
        out_shape=(
            jax.ShapeDtypeStruct((n, 256), bf16),
            jax.ShapeDtypeStruct((n, 64), bf16),
            jax.ShapeDtypeStruct((n, 64), bf16),
            jax.ShapeDtypeStruct((n // t, 64, t), bf16),
            jax.ShapeDtypeStruct((8, n), f32),
        ),
        grid=(n // tm,),
        in_specs=[
            pl.BlockSpec((tm, COL_BLK), lambda i: (i, CB_IDX)),
            pl.BlockSpec((1, 64), lambda i: (0, 0)),
            pl.BlockSpec((1, 64), lambda i: (0, 0)),
        ],
        out_specs=(row(256), row(64), row(64),
                   pl.BlockSpec((tm // t, 64, t), lambda i: (i, 0, 0)),
                   pl.BlockSpec((8, tm), lambda i: (0, i))),
        compiler_params=_cparams(("parallel",)),
        name="dsa_prep",
    )(proj, g, b)


HG_CHUNK = 32


def _split3(a):
    a1 = a.astype(bf16)
    r1 = a - a1.astype(f32)
    a2 = r1.astype(bf16)
    r2 = r1 - a2.astype(f32)
    return a1, a2, r2.astype(bf16)


def _hgrn2_kernel(q_ref, f_ref, i_ref, og_ref, lb_ref, gn_ref, o_ref, st_ref, oacc_ref, *, ts):
    c = HG_CHUNK

    @pl.when(pl.program_id(1) == 0)
    def _():
        st_ref[...] = jnp.zeros_like(st_ref)

    r_io = lax.broadcasted_iota(i32, (c, c), 0)
    c_io = lax.broadcasted_iota(i32, (c, c), 1)
    causal = c_io <= r_io
    tri = causal.astype(bf16)
    mid = c // 2 - 1

    heads = range(HG_HEADS)
    lanes = [slice(HG_D * h, HG_D * (h + 1)) for h in heads]

    def chunk_pair(cp, carry):
        chains = [(pl.ds(pl.multiple_of((2 * cp + u) * c, c), c), h) for u in range(2) for h in heads]
        q, k, v, b = [], [], [], []
        for rows, h in chains:
            qr = q_ref[rows, lanes[h]]
            q.append(qr * jax.nn.sigmoid(qr))
            lb = lb_ref[:, lanes[h]]
            fg = lb + (1.0 - lb) * jax.nn.sigmoid(f_ref[rows, lanes[h]])
            k.append(1.0 - fg)
            v.append(i_ref[rows, lanes[h]].astype(bf16))
            l1, l2, l3 = _split3(jnp.log(fg))
            b.append(jnp.dot(tri, l1, preferred_element_type=f32)
                     + jnp.dot(tri, l2, preferred_element_type=f32)
                     + jnp.dot(tri, l3, preferred_element_type=f32))
        a = []
        for n in range(len(chains)):
            b_mid = b[n][mid:mid + 1, :]
            qt = (q[n] * jnp.exp(b[n] - b_mid)).astype(bf16)
            kt = (k[n] * jnp.exp(b_mid - b[n])).astype(bf16)
            a.append(lax.dot_general(qt, kt, NT_DIMS, preferred_element_type=f32))
        intra = [jnp.dot(jnp.where(causal, a[n], 0.0).astype(bf16), v[n], preferred_element_type=f32)
                 for n in range(len(chains))]
        for n, (rows, h) in enumerate(chains):
            b_last = b[n][c - 1:c, :]
            st = st_ref[h]
            qe = (q[n] * jnp.exp(b[n])).astype(bf16)
            inter = lax.dot_general(qe, st.astype(bf16), NT_DIMS, preferred_element_type=f32)
            oacc_ref[rows, lanes[h]] = inter + intra[n]
            ke = (k[n] * jnp.exp(b_last - b[n])).astype(bf16)
            upd = lax.dot_general(v[n], ke, (((0,), (0,)), ((), ())), preferred_element_type=f32)
            st_ref[h] = st * jnp.exp(b_last) + upd
        return carry

    lax.fori_loop(0, ts // (2 * c), chunk_pair, 0)

    for h in range(HG_HEADS):
        sl = slice(HG_D * h, HG_D * (h + 1))
        o = oacc_ref[:, sl]
        ms = jnp.mean(o * o, axis=-1, keepdims=True)
        og = og_ref[:, sl]
        y = o * lax.rsqrt(ms + EPS) * gn_ref[:, sl] * (og * jax.nn.sigmoid(og))
        o_ref[:, sl] = y.astype(o_ref.dtype)


def _hgrn2(proj, lb, gn, batch, seq, ts=256):
    n = proj.shape[0]
    nsb = seq // ts
    col = lambda cb: pl.BlockSpec((ts, COL_BLK), lambda b, s, cb=cb: (b * nsb + s, cb))
    return pl.pallas_call(
        functools.partial(_hgrn2_kernel, ts=ts),
        out_shape=jax.ShapeDtypeStruct((n, HG_W), bf16),
        grid=(batch, nsb),
        in_specs=[col(CB_HQ), col(CB_HF), col(CB_HI), col(CB_HOG),
                  pl.BlockSpec((1, HG_W), lambda b, s: (0, 0)),
                  pl.BlockSpec((1, HG_W), lambda b, s: (0, 0))],
        out_specs=pl.BlockSpec((ts, HG_W), lambda b, s: (b * nsb + s, 0)),
        scratch_shapes=[pltpu.VMEM((HG_HEADS, HG_D, HG_D), f32), pltpu.VMEM((ts, HG_W), f32)],
        compiler_params=_cparams(("parallel", "arbitrary")),
        name="hgrn2",
    )(proj, proj, proj, proj, lb, gn)


LOG2E = 1.4426950408889634
FLT_MAX = 3.4028234663852886e38
NT_DIMS = (((1,), (1,)), ((), ()))


def _key_to_float(key):
    bits = jnp.where(key >= 0, key, key ^ jnp.int32(0x7FFFFFFF))
    return pltpu.bitcast(bits, f32)


def _dsa_kernel(iq_ref, wt_ref, aq_ref, ik_ref, ak_ref, avt_ref, bias_ref, o_ref,
                sc_ref, iqs_ref, qs_ref, lg_ref, pt_ref, al_ref, acc_ref, bef_ref, *, ktop):
    t = DSA_T
    qi = pl.program_id(1)
    npair = (qi + 2) // 2
    krow = lax.broadcasted_iota(i32, (t, t), 0)
    qcol = lax.broadcasted_iota(i32, (t, t), 1)
    qpos = qi * t + lax.broadcasted_iota(i32, (1, t), 1)

    iq = iq_ref[...]
    for j in range(IDX_HEADS):
        iqs_ref[t * j:t * (j + 1), :] = iq[:, IDX_DH * j:IDX_DH * (j + 1)]
    aq = aq_ref[...] * (ATT_DH ** -0.5 * LOG2E)
    for h in range(ATT_HEADS):
        qs_ref[t * h:t * (h + 1), :] = aq[:, ATT_DH * h:ATT_DH * (h + 1)].astype(bf16)
    wt = wt_ref[...]

    def score_pair(kp):
        ksl = pl.ds(pl.multiple_of(kp * 2 * t, 2 * t), 2 * t)
        rel = lax.dot_general(ik_ref[ksl, :], iqs_ref[...], NT_DIMS, preferred_element_type=f32)
        for u in range(2):
            kb = 2 * kp + u
            s = jnp.zeros((t, t), f32)
            for j in range(IDX_HEADS):
                s = s + wt[j:j + 1, :] * jnp.maximum(rel[t * u:t * (u + 1), t * j:t * (j + 1)], 0.0)
            sc_ref[kb] = jnp.where((kb * t + krow) <= (qi * t + qcol), s, -jnp.inf)

    def score_quad(kq, carry):
        score_pair(2 * kq)
        score_pair(2 * kq + 1)
        return carry

    lax.fori_loop(0, (qi + 4) // 4, score_quad, 0)

    def count_keys(pred):
        def body(kp, acc):
            acc = acc + pred(sc_ref[2 * kp], 2 * kp).astype(i32)
            return acc + pred(sc_ref[2 * kp + 1], 2 * kp + 1).astype(i32)
        acc = lax.fori_loop(0, npair, body, jnp.zeros((t, t), i32))
        return jnp.sum(acc, axis=0, keepdims=True)

    take_all = qpos < ktop
    few_pos = count_keys(lambda s, kb: s > 0.0) < ktop

    def unsettled(key, cnt):
        done = take_all | (cnt == ktop) | (few_pos & (key == 0))
        return jnp.min(done.astype(i32)) == 0

    bits_per_check = 4

    def bit_cond(c):
        return (c[0] < 32 // bits_per_check) & unsettled(c[1], c[2])

    def bit_step(c):
        g, key, cnt = c
        for b in range(bits_per_check):
            cand = key + jnp.left_shift(jnp.int32(1), 31 - b - bits_per_check * g)
            cf = _key_to_float(cand)
            cc = count_keys(lambda s, kb: s >= cf)
            ok = cc >= ktop
            key = jnp.where(ok, cand, key)
            cnt = jnp.where(ok, cc, cnt)
        return g + 1, key, cnt

    _, key, cnt = lax.while_loop(bit_cond, bit_step,
                                 (jnp.int32(0), jnp.full((1, t), -2 ** 31, i32), qpos + 1))
    thr = jnp.where(take_all, -FLT_MAX, _key_to_float(key))

    @pl.when(jnp.max(jnp.where(take_all, ktop, cnt)) > ktop)
    def _():
        def tally(kb, carry):
            gt, before = carry
            s = sc_ref[kb]
            bef_ref[kb] = before
            ties = jnp.sum(jnp.where(s == thr, 1.0, 0.0), axis=0, keepdims=True)
            return gt + jnp.where(s > thr, 1.0, 0.0), before + ties

        zero = jnp.zeros((1, t), f32)
        nquad = (qi + 4) // 4
        gt, _ = lax.fori_loop(0, 4 * nquad, tally, (jnp.zeros((t, t), f32), zero))
        need = ktop - jnp.sum(gt, axis=0, keepdims=True)
        tri = (qcol <= krow).astype(bf16)

        def drop(kb):
            s = sc_ref[kb]
            tie = s == thr
            run = bef_ref[kb] + jnp.dot(tri, jnp.where(tie, 1.0, 0.0).astype(bf16), preferred_element_type=f32)
            sc_ref[kb] = jnp.where(tie & (run > need), -jnp.inf, s)

        def drop_quad(kq, carry):
            for u in range(4):
                drop(4 * kq + u)
            return carry

        lax.fori_loop(0, nquad, drop_quad, 0)

    nq = sc_ref.shape[0]
    hw = ATT_HEADS * t
    acc_ref[...] = jnp.zeros(acc_ref.shape, f32)
    pt_ref[1] = jnp.zeros(pt_ref.shape[1:], bf16)
    al_ref[1] = jnp.ones(al_ref.shape[1:], f32)

    def logit_stage(j, par):
        for u in range(2):
            kb = jnp.minimum(2 * j + u, nq - 1)
            kblk = ak_ref[pl.ds(pl.multiple_of(kb * t, t), t), :]
            lg_ref[par, u] = lax.dot_general(kblk, qs_ref[...], NT_DIMS, preferred_element_type=f32)

    def softmax_stage(j, par, m_all, l_all):
        for u in range(2):
            kb = 2 * j + u
            madd = jnp.where(sc_ref[kb] >= thr, 0.0, NEG)
            near = jnp.clip(qi - kb, 0, 2)
            ms, ls, als = [], [], []
            for h in range(ATT_HEADS):
                hs = slice(t * h, t * (h + 1))
                lg = lg_ref[par, u, :, hs] + (madd + bias_ref[near, h])
                m_old = m_all[:, hs]
                m_new = jnp.maximum(m_old, jnp.max(lg, axis=0, keepdims=True))
                alpha = jnp.exp2(m_old - m_new)
                p = jnp.exp2(lg - m_new)
                ls.append(alpha * l_all[:, hs] + jnp.sum(p, axis=0, keepdims=True))
                ms.append(m_new)
                als.append(alpha)
                pt_ref[par, u, :, hs] = p.astype(bf16)
            al_ref[par, u] = jnp.concatenate(als, axis=1)
            m_all, l_all = jnp.concatenate(ms, axis=1), jnp.concatenate(ls, axis=1)
        return m_all, l_all

    def value_stage(j, par):
        for u in range(2):
            kb = jnp.maximum(2 * j + u, 0)
            pv = jnp.dot(avt_ref[kb], pt_ref[par, u], preferred_element_type=f32)
            acc_ref[...] = acc_ref[...] * al_ref[par, u] + pv

    def attn_trip(q, carry):
        for par in range(2):
            j = 2 * q + par
            value_stage(j - 1, 1 - par)
            carry = softmax_stage(j, par, *carry)
            logit_stage(j + 1, 1 - par)
        return carry

    def odd_tail(carry):
        j = 2 * (npair // 2)
        value_stage(j - 1, 1)
        carry = softmax_stage(j, 0, *carry)
        value_stage(j, 0)
        return carry

    def even_tail(carry):
        value_stage(2 * (npair // 2) - 1, 1)
        return carry

    logit_stage(0, 0)
    carry = (jnp.full((1, hw), NEG, f32), jnp.zeros((1, hw), f32))
    carry = lax.fori_loop(0, npair // 2, attn_trip, carry)
    _, l_all = lax.cond(npair % 2 == 1, odd_tail, even_tail, carry)

    out_t = acc_ref[...] / l_all
    out_t = jnp.concatenate([out_t[:, t * h:t * (h + 1)] for h in range(ATT_HEADS)], axis=0)
    o_ref[...] = out_t.T.astype(o_ref.dtype)


def _rel_bias_tiles(rel_bias):
    t = DSA_T
    ki = jnp.arange(t)[:, None]
    qj = jnp.arange(t)[None, :]
    far = rel_bias[REL_BUCKETS - 1].astype(f32)
    tiles = []
    for v in range(2):
        dist = jnp.maximum(v * t + qj - ki, 0)
        max_exact = REL_BUCKETS // 2
        nf = jnp.maximum(dist, 1).astype(f32)
        large = max_exact + (jnp.log(nf / max_exact) / math.log(REL_MAX_DIST / max_exact)
                             * (REL_BUCKETS - max_exact)).astype(i32)
        large = jnp.minimum(large, REL_BUCKETS - 1)
        bucket = jnp.where(dist < max_exact, dist, large)
        tiles.append(jnp.transpose((rel_bias[bucket].astype(f32) - far) * LOG2E, (2, 0, 1)))
    tiles.append(jnp.zeros_like(tiles[0]))
    return jnp.stack(tiles)


def _dsa(proj, iq, wt, ik, ak, avt, bias_tiles, batch, seq):
    n = proj.shape[0]
    t = DSA_T
    nq = seq // t
    assert nq % 4 == 0 and 2 * t >= REL_MAX_DIST
    ktop = min(TOPK_MAX, seq // 4)
    qrow = lambda width: pl.BlockSpec((t, width), lambda b, q: (b * nq + q, 0))
    kv = pl.BlockSpec((seq, 64), lambda b, q: (b, 0))
    hw = ATT_HEADS * t
    return pl.pallas_call(
        functools.partial(_dsa_kernel, ktop=ktop),
        out_shape=jax.ShapeDtypeStruct((n, ATT_W), bf16),
        grid=(batch, nq),
        in_specs=[qrow(256),
                  pl.BlockSpec((8, t), lambda b, q: (0, b * nq + q)),
                  pl.BlockSpec((t, COL_BLK), lambda b, q: (b * nq + q, CB_AQ)),
                  kv, kv,
                  pl.BlockSpec((nq, 64, t), lambda b, q: (b, 0, 0)),
                  pl.BlockSpec((3, ATT_HEADS, t, t), lambda b, q: (0, 0, 0, 0))],
        out_specs=qrow(ATT_W),
        scratch_shapes=[pltpu.VMEM((nq, t, t), f32),
                        pltpu.VMEM((IDX_HEADS * t, IDX_DH), bf16),
                        pltpu.VMEM((hw, ATT_DH), bf16),
                        pltpu.VMEM((2, 2, t, hw), f32),
                        pltpu.VMEM((2, 2, t, hw), bf16),
                        pltpu.VMEM((2, 2, 1, hw), f32),
                        pltpu.VMEM((ATT_DH, hw), f32),
                        pltpu.VMEM((nq, 1, t), f32)],
        compiler_params=_cparams(("parallel", "arbitrary")),
        name="dsa",
    )(iq, wt, proj, ik, ak, avt, bias_tiles)


def _merge_kernel(x_ref, ya_ref, yb_ref, ga_ref, gb_ref, wa_ref, wb_ref, wo_ref, g2_ref, wq_ref,
                  x1_ref, xnt_ref, qp_ref):
    ha = jnp.dot(ya_ref[...], wa_ref[...], preferred_element_type=f32)
    hb = jnp.dot(yb_ref[...], wb_ref[...], preferred_element_type=f32)
    h = jax.nn.sigmoid(ga_ref[...]) * ha + jax.nn.sigmoid(gb_ref[...]) * hb
    x1 = x_ref[...] + jnp.dot(h.astype(bf16), wo_ref[...], preferred_element_type=f32)
    x1_ref[...] = x1
    ms = jnp.mean(x1 * x1, axis=-1, keepdims=True)
    xn = x1 * lax.rsqrt(ms + EPS) * g2_ref[...]
    xnt_ref[...] = xn.T.astype(bf16)
    qp_ref[...] = jnp.dot(xn.astype(bf16), wq_ref[...], preferred_element_type=f32).astype(bf16)


def _merge(x2, ya, yb, proj, wa, wb, wo, g2, wq, tm=512):
    n = x2.shape[0]
    qw = wq.shape[1]
    row = lambda width: pl.BlockSpec((tm, width), lambda i: (i, 0))
    full = lambda a: pl.BlockSpec(a.shape, lambda i: (0, 0))
    return pl.pallas_call(
        _merge_kernel,
        out_shape=(jax.ShapeDtypeStruct((n, D_MODEL), f32),
                   jax.ShapeDtypeStruct((D_MODEL, n), bf16),
                   jax.ShapeDtypeStruct((n, qw), bf16)),
        grid=(n // tm,),
        in_specs=[row(D_MODEL), row(HG_W), row(ATT_W),
                  pl.BlockSpec((tm, D_MODEL), lambda i: (i, CB_GA)),
                  pl.BlockSpec((tm, D_MODEL), lambda i: (i, CB_GB)),
                  full(wa), full(wb), full(wo), full(g2), full(wq)],
        out_specs=(row(D_MODEL), pl.BlockSpec((D_MODEL, tm), lambda i: (0, i)), row(qw)),
        compiler_params=_cparams(("parallel",)),
        name="merge",
    )(x2, ya, yb, proj, proj, wa, wb, wo, g2, wq)


_PAIR_COUNTS = tuple(PEER_TOPK // (p + 1) for p in range(PEER_TOPK))
_NCAND = sum(_PAIR_COUNTS)
F32_SUBLANES = 8


def _exchange(v, i, j):
    v[i], v[j] = jnp.maximum(v[i], v[j]), jnp.minimum(v[i], v[j])


def _bitonic_sort(v):
    n = len(v)
    k = 2
    while k <= n:
        j = k // 2
        while j >= 1:
            for i in range(n):
                l = i ^ j
                if l > i:
                    if i & k:
                        _exchange(v, l, i)
                    else:
                        _exchange(v, i, l)
            j //= 2
        k *= 2


def _bitonic_merge(v):
    j = len(v) // 2
    while j >= 1:
        for i in range(len(v)):
            if i ^ j > i:
                _exchange(v, i, i ^ j)
        j //= 2


def _merge_sublanes(v, shifts):
    for shift in shifts:
        n = len(v)
        v = [jnp.maximum(v[g], pltpu.roll(v[n - 1 - g], shift, axis=0)) for g in range(n)]
        _bitonic_merge(v)
    return v


def _top_sorted(s):
    v = [s[F32_SUBLANES * g:F32_SUBLANES * (g + 1), :] for g in range(s.shape[0] // F32_SUBLANES)]
    _bitonic_sort(v)
    return _merge_sublanes(v[:PEER_TOPK], (4, 2, 1))


def _peer_route_kernel(qp_ref, k1_ref, k2_ref, n1_ref, a1_ref, r2_ref, e2_ref):
    tb = qp_ref.shape[0]
    sub = lax.broadcasted_iota(i32, (F32_SUBLANES, tb), 0)
    groups = PEER_NKEYS // F32_SUBLANES
    for h in range(PEER_HEADS):
        q1 = qp_ref[:, 2 * PEER_HALF * h:2 * PEER_HALF * h + PEER_HALF]
        q2 = qp_ref[:, 2 * PEER_HALF * h + PEER_HALF:2 * PEER_HALF * (h + 1)]
        s1 = lax.dot_general(k1_ref[h], q1, NT_DIMS, preferred_element_type=f32)
        s2 = lax.dot_general(k2_ref[h], q2, NT_DIMS, preferred_element_type=f32)
        a = _top_sorted(s1)
        b = _top_sorted(s2)
        nreg = pl.next_power_of_2(pl.cdiv(_NCAND, F32_SUBLANES))
        cand = [jnp.full((F32_SUBLANES, tb), -jnp.inf, f32) for _ in range(nreg)]
        r = 0
        for p in range(PEER_TOPK):
            for q in range(_PAIR_COUNTS[p]):
                cand[r // F32_SUBLANES] = jnp.where(sub == r % F32_SUBLANES, a[p] + b[q], cand[r // F32_SUBLANES])
                r += 1
        _bitonic_sort(cand)
        top = cand + [pltpu.roll(cand[len(cand) - 1 - g], 4, axis=0) for g in range(len(cand))]
        _bitonic_merge(top)
        top = _merge_sublanes(top, (2, 1))
        thr = top[PEER_TOPK - 1]
        mx = a[0] + b[0]
        z = jnp.zeros_like(mx)
        for tv in top:
            z = z + jnp.exp(tv - mx)
        s1g = s1.reshape(groups, F32_SUBLANES, tb)
        s2g = s2.reshape(groups, F32_SUBLANES, tb)
        n1 = jnp.zeros_like(s1g)
        r2 = jnp.zeros_like(s2g)
        for q in range(PEER_TOPK):
            n1 = n1 + jnp.where(s1g + b[q] >= thr, 1.0, 0.0)
            r2 = r2 + jnp.where(b[q] > s2g, 1.0, 0.0)
        n1_ref[h] = n1.reshape(PEER_NKEYS, tb)
        a1_ref[h] = (jnp.exp(s1g - a[0]) / z).reshape(PEER_NKEYS, tb)
        r2_ref[h] = r2.reshape(PEER_NKEYS, tb).astype(bf16)
        e2_ref[h] = jnp.exp(s2g - b[0]).reshape(PEER_NKEYS, tb).astype(bf16)


def _peer_route(qp, k1, k2, tb=512):
    n = qp.shape[0]
    big = lambda: pl.BlockSpec((PEER_HEADS, PEER_NKEYS, tb), lambda i: (0, 0, i))
    bigs = lambda dt: jax.ShapeDtypeStruct((PEER_HEADS, PEER_NKEYS, n), dt)
    return pl.pallas_call(
        _peer_route_kernel,
        out_shape=(bigs(f32), bigs(f32), bigs(bf16), bigs(bf16)),
        grid=(n // tb,),
        in_specs=[pl.BlockSpec((tb, qp.shape[1]), lambda i: (i, 0)),
                  pl.BlockSpec(k1.shape, lambda i: (0, 0, 0)),
                  pl.BlockSpec(k2.shape, lambda i: (0, 0, 0))],
        out_specs=(big(), big(), big(), big()),
        compiler_params=_cparams(("parallel",)),
        name="peer_route",
    )(qp, k1, k2)


PEER_EB = 512
PEER_NB = 4


def _peer_dense_kernel(xt_ref, u_ref, vt_ref, n1_ref, a1_ref, r2_ref, e2_ref, x1_ref, gf_ref,
                       o_ref, acc_ref, act_ref, g_ref, *, final_norm):
    e = pl.program_id(1)
    tb = xt_ref.shape[1]

    @pl.when(e == 0)
    def _():
        acc_ref[...] = jnp.zeros_like(acc_ref)

    nblk = PEER_EB // PEER_NKEYS
    reps = PEER_NKEYS // BF16_SUBLANES

    def gates(blk, slot):
        for ii in range(nblk):
            i = jnp.minimum(blk * nblk + ii, PEER_NKEYS - 1)
            g = jnp.zeros((PEER_NKEYS, tb), bf16)
            for h in range(PEER_HEADS):
                n_row = jnp.broadcast_to(n1_ref[h, pl.ds(i, 1), :], (BF16_SUBLANES, tb)).astype(bf16)
                a_row = jnp.broadcast_to(a1_ref[h, pl.ds(i, 1), :], (BF16_SUBLANES, tb)).astype(bf16)
                n_all = jnp.tile(n_row, (reps, 1))
                a_all = jnp.tile(a_row, (reps, 1))
                g = g + jnp.where(r2_ref[h] < n_all, e2_ref[h] * a_all, jnp.zeros((), bf16))
            g_ref[slot, PEER_NKEYS * ii:PEER_NKEYS * (ii + 1), :] = g

    def hidden(slot):
        return jnp.dot(u_ref[PEER_EB * slot:PEER_EB * (slot + 1), :], xt_ref[...], preferred_element_type=f32)

    def outputs(slot, hh):
        act = 0.5 * hh * (1.0 + lax.erf(hh * (2.0 ** -0.5)))
        act_ref[slot] = (act * g_ref[slot].astype(f32)).astype(bf16)
        acc_ref[...] += jnp.dot(vt_ref[:, PEER_EB * slot:PEER_EB * (slot + 1)], act_ref[slot],
                                preferred_element_type=f32)

    @pl.when(e == 0)
    def _():
        for slot in range(1, PEER_NB):
            gates(slot, slot)

    gates(PEER_NB * e, 0)
    hts = [hidden(slot) for slot in range(PEER_NB)]
    for slot in range(PEER_NB):
        outputs(slot, hts[slot])
    for slot in range(1, PEER_NB):
        gates(PEER_NB * (e + 1) + slot, slot)

    @pl.when(e == pl.num_programs(1) - 1)
    def _():
        y = x1_ref[...] + acc_ref[...].T
        if final_norm:
            ms = jnp.mean(y * y, axis=-1, keepdims=True)
            y = y * lax.rsqrt(ms + EPS) * gf_ref[...]
        o_ref[...] = y


def _peer_dense(xt, u, vt, n1, a1, r2, e2, x1, gf, final_norm, tb=512):
    n = xt.shape[1]
    ne = u.shape[0]
    step = PEER_NB * PEER_EB
    big = lambda: pl.BlockSpec((PEER_HEADS, PEER_NKEYS, tb), lambda t, e: (0, 0, t))
    return pl.pallas_call(
        functools.partial(_peer_dense_kernel, final_norm=final_norm),
        out_shape=jax.ShapeDtypeStruct((n, D_MODEL), f32),
        grid=(n // tb, ne // step),
        in_specs=[pl.BlockSpec((D_MODEL, tb), lambda t, e: (0, t)),
                  pl.BlockSpec((step, D_MODEL), lambda t, e: (e, 0)),
                  pl.BlockSpec((D_MODEL, step), lambda t, e: (0, e)),
                  big(), big(), big(), big(),
                  pl.BlockSpec((tb, D_MODEL), lambda t, e: (t, 0)),
                  pl.BlockSpec((1, D_MODEL), lambda t, e: (0, 0))],
        out_specs=pl.BlockSpec((tb, D_MODEL), lambda t, e: (t, 0)),
        scratch_shapes=[pltpu.VMEM((D_MODEL, tb), f32), pltpu.VMEM((PEER_NB, PEER_EB, tb), bf16),
                        pltpu.VMEM((PEER_NB, PEER_EB, tb), bf16)],
        compiler_params=_cparams(("parallel", "arbitrary")),
        name="peer_dense",
    )(xt, u, vt, n1, a1, r2, e2, x1, gf)


def _pack_w_in(w):
    offs = np.concatenate([[0], np.cumsum(np.array(IN_SIZES))]).tolist()
    seg = lambda k: w[:, offs[k]:offs[k + 1]]
    hq, hf, hi, hog, aq, ak, av, iq, ik, iw, ga, gb = [seg(k) for k in range(12)]
    pad = jnp.zeros((w.shape[0], 64 - IDX_HEADS), w.dtype)
    return jnp.concatenate([hq, hf, hi, hog, aq, iq, ak, av, ik, iw, pad, ga, gb], axis=1).astype(bf16)


def _layer(x2, batch, seq, norm_mix, w_in, lb, hg_norm, ikg, ikb, bias_tiles, w_up_a, w_up_b, w_out,
           norm_ffn, peer_wq, peer_keys, peer_u, peer_v):
    proj = _inproj(x2, norm_mix.reshape(1, -1), _pack_w_in(w_in))
    iq, ak, ik, avt, wt = _dsa_prep(proj, ikg.reshape(1, -1), ikb.reshape(1, -1))
    ya = _hgrn2(proj, lb.reshape(1, -1), hg_norm.reshape(1, -1), batch, seq)
    yb = _dsa(proj, iq, wt, ik, ak, avt, bias_tiles, batch, seq)
    x1, xn, qp = _merge(x2, ya, yb, proj, w_up_a.astype(bf16), w_up_b.astype(bf16), w_out.astype(bf16),
                        norm_ffn.reshape(1, -1), peer_wq.astype(bf16))
    route = _peer_route(qp, peer_keys[0].astype(bf16), peer_keys[1].astype(bf16))
    return xn, route, x1


def kernel(x, norm_mix, w_in, hg_lb, hg_norm, idx_k_norm_g, idx_k_norm_b, rel_bias, w_up_a, w_up_b, w_out,
           norm_ffn, peer_wq, peer_keys, peer_u, peer_v, norm_final):
    batch, seq, _ = x.shape
    depth = w_in.shape[0]
    lb_all = jnp.cumsum(jax.nn.softmax(hg_lb.astype(f32), axis=0), axis=0)
    bias_tiles = _rel_bias_tiles(rel_bias)
    x2 = x.reshape(batch * seq, D_MODEL)
    gf = norm_final.reshape(1, -1)
    for l in range(depth):
        xn, route, x1 = _layer(
            x2, batch, seq, norm_mix[l], w_in[l], lb_all[l], hg_norm[l], idx_k_norm_g[l], idx_k_norm_b[l],
            bias_tiles, w_up_a[l], w_up_b[l], w_out[l], norm_ffn[l], peer_wq[l], peer_keys[l], peer_u[l], peer_v[l])
        x2 = _peer_dense(xn, peer_u[l].astype(bf16), peer_v[l].astype(bf16).T, *route, x1, gf,
                         final_norm=(l == depth - 1))
    return x2.reshape(batch, seq, D_MODEL)
```

```python
import functools
import math

import jax
import jax.numpy as jnp
import numpy as np
from jax import lax
from jax.experimental import pallas as pl
from jax.experimental.pallas import tpu as pltpu

f32 = jnp.float32
bf16 = jnp.bfloat16
i32 = jnp.int32

D_MODEL = 1024
EPS = 1e-6
HG_HEADS, HG_D = 4, 128
HG_W = HG_HEADS * HG_D
ATT_HEADS, ATT_DH = 8, 64
ATT_W = ATT_HEADS * ATT_DH
IDX_HEADS, IDX_DH = 4, 64
TOPK_MAX = 256
REL_BUCKETS, REL_MAX_DIST = 32, 128
PEER_HEADS, PEER_NKEYS, PEER_HALF, PEER_TOPK = 8, 128, 128, 16
PEER_EXPERTS = PEER_NKEYS * PEER_NKEYS
IN_SIZES = (HG_W, HG_W, HG_W, HG_W, ATT_W, ATT_DH, ATT_DH, IDX_HEADS * IDX_DH, IDX_DH, IDX_HEADS, D_MODEL, D_MODEL)

LANES = 128
BF16_SUBLANES = 16
VMEM_LIMIT = 56 * 1024 * 1024

COL_BLK = 512
PROJ_W = 10 * COL_BLK
CB_HQ, CB_HF, CB_HI, CB_HOG, CB_AQ, CB_IDX = 0, 1, 2, 3, 4, 5
CB_GA, CB_GB = 3, 4
NEG = -1e30


def _cparams(sem):
    return pltpu.CompilerParams(dimension_semantics=sem, vmem_limit_bytes=VMEM_LIMIT)


def _inproj_kernel(x_ref, g_ref, w_ref, o_ref, xn_ref):
    @pl.when(pl.program_id(1) == 0)
    def _():
        x = x_ref[...]
        ms = jnp.mean(x * x, axis=-1, keepdims=True)
        xn_ref[...] = (x * lax.rsqrt(ms + EPS) * g_ref[...]).astype(bf16)

    o_ref[...] = jnp.dot(xn_ref[...], w_ref[...], preferred_element_type=f32)


def _inproj(x2, g, w, tm=1024):
    n = x2.shape[0]
    return pl.pallas_call(
        _inproj_kernel,
        out_shape=jax.ShapeDtypeStruct((n, PROJ_W), f32),
        grid=(n // tm, PROJ_W // COL_BLK),
        in_specs=[
            pl.BlockSpec((tm, D_MODEL), lambda i, j: (i, 0)),
            pl.BlockSpec((1, D_MODEL), lambda i, j: (0, 0)),
            pl.BlockSpec((D_MODEL, COL_BLK), lambda i, j: (0, j)),
        ],
        out_specs=pl.BlockSpec((tm, COL_BLK), lambda i, j: (i, j)),
        scratch_shapes=[pltpu.VMEM((tm, D_MODEL), bf16)],
        compiler_params=_cparams(("parallel", "arbitrary")),
        name="inproj",
    )(x2, g, w)


DSA_T = 128


def _dsa_prep_kernel(p_ref, g_ref, b_ref, iq_ref, ak_ref, ik_ref, avt_ref, wt_ref):
    t = DSA_T
    p = p_ref[...]
    iq_ref[...] = p[:, 0:256].astype(bf16)
    ak_ref[...] = p[:, 256:320].astype(bf16)
    ik = p[:, 384:448]
    mu = jnp.mean(ik, axis=-1, keepdims=True)
    var = jnp.mean(jnp.square(ik - mu), axis=-1, keepdims=True)
    y = (ik - mu) * lax.rsqrt(var + EPS) * g_ref[...] + b_ref[...]
    ik_ref[...] = y.astype(bf16)
    kv_t = p[:, 256:384].T
    for c in range(p.shape[0] // t):
        avt_ref[c] = kv_t[64:128, t * c:t * (c + 1)].astype(bf16)
    wt_ref[...] = p[:, 384:512].T[64:72, :] * (IDX_HEADS ** -0.5 * IDX_DH ** -0.5)


def _dsa_prep(proj, g, b, tm=1024):
    n = proj.shape[0]
    t = DSA_T
    row = lambda w: pl.BlockSpec((tm, w), lambda i: (i, 0))
    return pl.pallas_call(
        _dsa_prep_kernel,
        out_shape=(
            jax.ShapeDtypeStruct((n, 256), bf16),
            jax.ShapeDtypeStruct((n, 64), bf16),
            jax.ShapeDtypeStruct((n, 64), bf16),
            jax.ShapeDtypeStruct((n // t, 64, t), bf16),
            jax.ShapeDtypeStruct((8, n), f32),
        ),
        grid=(n // tm,),
        in_specs=[
            pl.BlockSpec((tm, COL_BLK), lambda i: (i, CB_IDX)),
            pl.BlockSpec((1, 64), lambda i: (0, 0)),
            pl.BlockSpec((1, 64), lambda i: (0, 0)),
        ],
        out_specs=(row(256), row(64), row(64),
                   pl.BlockSpec((tm // t, 64, t), lambda i: (i, 0, 0)),
                   pl.BlockSpec((8, tm), lambda i: (0, i))),
        compiler_params=_cparams(("parallel",)),
        name="dsa_prep",
    )(proj, g, b)


HG_CHUNK = 32


def _split3(a):
    a1 = a.astype(bf16)
    r1 = a - a1.astype(f32)
    a2 = r1.astype(bf16)
    r2 = r1 - a2.astype(f32)
    return a1, a2, r2.astype(bf16)


def _hgrn2_kernel(q_ref, f_ref, i_ref, og_ref, lb_ref, gn_ref, o_ref, st_ref, oacc_ref, *, ts):
    c = HG_CHUNK

    @pl.when(pl.program_id(1) == 0)
    def _():
        st_ref[...] = jnp.zeros_like(st_ref)

    r_io = lax.broadcasted_iota(i32, (c, c), 0)
    c_io = lax.broadcasted_iota(i32, (c, c), 1)
    causal = c_io <= r_io
    tri = causal.astype(bf16)
    mid = c // 2 - 1

    heads = range(HG_HEADS)
    lanes = [slice(HG_D * h, HG_D * (h + 1)) for h in heads]

    def chunk_pair(cp, carry):
        chains = [(pl.ds(pl.multiple_of((2 * cp + u) * c, c), c), h) for u in range(2) for h in heads]
        q, k, v, b = [], [], [], []
        for rows, h in chains:
            qr = q_ref[rows, lanes[h]]
            q.append(qr * jax.nn.sigmoid(qr))
            lb = lb_ref[:, lanes[h]]
            fg = lb + (1.0 - lb) * jax.nn.sigmoid(f_ref[rows, lanes[h]])
            k.append(1.0 - fg)
            v.append(i_ref[rows, lanes[h]].astype(bf16))
            l1, l2, l3 = _split3(jnp.log(fg))
            b.append(jnp.dot(tri, l1, preferred_element_type=f32)
                     + jnp.dot(tri, l2, preferred_element_type=f32)
                     + jnp.dot(tri, l3, preferred_element_type=f32))
        a = []
        for n in range(len(chains)):
            b_mid = b[n][mid:mid + 1, :]
            qt = (q[n] * jnp.exp(b[n] - b_mid)).astype(bf16)
            kt = (k[n] * jnp.exp(b_mid - b[n])).astype(bf16)
            a.append(lax.dot_general(qt, kt, NT_DIMS, preferred_element_type=f32))
        intra = [jnp.dot(jnp.where(causal, a[n], 0.0).astype(bf16), v[n], preferred_element_type=f32)
                 for n in range(len(chains))]
        for n, (rows, h) in enumerate(chains):
            b_last = b[n][c - 1:c, :]
            st = st_ref[h]
            qe = (q[n] * jnp.exp(b[n])).astype(bf16)
            inter = lax.dot_general(qe, st.astype(bf16), NT_DIMS, preferred_element_type=f32)
            oacc_ref[rows, lanes[h]] = inter + intra[n]
            ke = (k[n] * jnp.exp(b_last - b[n])).astype(bf16)
            upd = lax.dot_general(v[n], ke, (((0,), (0,)), ((), ())), preferred_element_type=f32)
            st_ref[h] = st * jnp.exp(b_last) + upd
        return carry

    lax.fori_loop(0, ts // (2 * c), chunk_pair, 0)

    for h in range(HG_HEADS):
        sl = slice(HG_D * h, HG_D * (h + 1))
        o = oacc_ref[:, sl]
        ms = jnp.mean(o * o, axis=-1, keepdims=True)
        og = og_ref[:, sl]
        y = o * lax.rsqrt(ms + EPS) * gn_ref[:, sl] * (og * jax.nn.sigmoid(og))
        o_ref[:, sl] = y.astype(o_ref.dtype)


def _hgrn2(proj, lb, gn, batch, seq, ts=256):
    n = proj.shape[0]
    nsb = seq // ts
    col = lambda cb: pl.BlockSpec((ts, COL_BLK), lambda b, s, cb=cb: (b * nsb + s, cb))
    return pl.pallas_call(
        functools.partial(_hgrn2_kernel, ts=ts),
        out_shape=jax.ShapeDtypeStruct((n, HG_W), bf16),
        grid=(batch, nsb),
        in_specs=[col(CB_HQ), col(CB_HF), col(CB_HI), col(CB_HOG),
                  pl.BlockSpec((1, HG_W), lambda b, s: (0, 0)),
                  pl.BlockSpec((1, HG_W), lambda b, s: (0, 0))],
        out_specs=pl.BlockSpec((ts, HG_W), lambda b, s: (b * nsb + s, 0)),
        scratch_shapes=[pltpu.VMEM((HG_HEADS, HG_D, HG_D), f32), pltpu.VMEM((ts, HG_W), f32)],
        compiler_params=_cparams(("parallel", "arbitrary")),
        name="hgrn2",
    )(proj, proj, proj, proj, lb, gn)


LOG2E = 1.4426950408889634
FLT_MAX = 3.4028234663852886e38
NT_DIMS = (((1,), (1,)), ((), ()))


def _key_to_bits(key):
    return jnp.where(key >= 0, key, key ^ jnp.int32(0x7FFFFFFF))


def _key_to_float(key):
    return pltpu.bitcast(_key_to_bits(key), f32)


def _upper_half(bits):
    return pltpu.bitcast(bits & jnp.int32(-65536), f32).astype(bf16)


def _dsa_kernel(iq_ref, wt_ref, aq_ref, ik_ref, ak_ref, avt_ref, bias_ref, o_ref,
                sc_ref, sch_ref, iqs_ref, qs_ref, lg_ref, pt_ref, al_ref, acc_ref, bef_ref, *, ktop):
    t = DSA_T
    qi = pl.program_id(1)
    npair = (qi + 2) // 2
    krow = lax.broadcasted_iota(i32, (t, t), 0)
    qcol = lax.broadcasted_iota(i32, (t, t), 1)
    qpos = qi * t + lax.broadcasted_iota(i32, (1, t), 1)

    iq = iq_ref[...]
    for j in range(IDX_HEADS):
        iqs_ref[t * j:t * (j + 1), :] = iq[:, IDX_DH * j:IDX_DH * (j + 1)]
    aq = aq_ref[...] * (ATT_DH ** -0.5 * LOG2E)
    for h in range(ATT_HEADS):
        qs_ref[t * h:t * (h + 1), :] = aq[:, ATT_DH * h:ATT_DH * (h + 1)].astype(bf16)
    wt = wt_ref[...]

    def score_pair(kp):
        ksl = pl.ds(pl.multiple_of(kp * 2 * t, 2 * t), 2 * t)
        rel = lax.dot_general(ik_ref[ksl, :], iqs_ref[...], NT_DIMS, preferred_element_type=f32)
        for u in range(2):
            kb = 2 * kp + u
            s = jnp.zeros((t, t), f32)
            for j in range(IDX_HEADS):
                s = s + wt[j:j + 1, :] * jnp.maximum(rel[t * u:t * (u + 1), t * j:t * (j + 1)], 0.0)
            s = jnp.where((kb * t + krow) <= (qi * t + qcol), s, -jnp.inf)
            sc_ref[kb] = s
            sch_ref[kb] = _upper_half(pltpu.bitcast(s, i32))

    def score_quad(kq, carry):
        score_pair(2 * kq)
        score_pair(2 * kq + 1)
        return carry

    lax.fori_loop(0, (qi + 4) // 4, score_quad, 0)

    def count_keys(pred):
        def body(kp, acc):
            acc = acc + pred(sc_ref[2 * kp], 2 * kp).astype(i32)
            return acc + pred(sc_ref[2 * kp + 1], 2 * kp + 1).astype(i32)
        acc = lax.fori_loop(0, npair, body, jnp.zeros((t, t), i32))
        return jnp.sum(acc, axis=0, keepdims=True)

    take_all = qpos < ktop
    few_pos = count_keys(lambda s, kb: s > 0.0) < ktop

    def unsettled(key, cnt):
        done = take_all | (cnt == ktop) | (few_pos & (key == 0))
        return jnp.min(done.astype(i32)) == 0

    bits_per_check = 4

    def count_coarse(cfb):
        one, nil = jnp.ones((), bf16), jnp.zeros((), bf16)

        def body(kp, acc):
            acc = acc + jnp.where(sch_ref[2 * kp] >= cfb, one, nil)
            return acc + jnp.where(sch_ref[2 * kp + 1] >= cfb, one, nil)
        acc = lax.fori_loop(0, npair, body, jnp.zeros((t, t), bf16))
        return jnp.sum(acc.astype(f32), axis=0, keepdims=True).astype(i32)

    def make_step(first_group, count_ge):
        def bit_step(c):
            g, key, cnt = c
            for b in range(bits_per_check):
                cand = key + jnp.left_shift(jnp.int32(1), 31 - b - bits_per_check * (g + first_group))
                cc = count_ge(cand)
                ok = cc >= ktop
                key = jnp.where(ok, cand, key)
                cnt = jnp.where(ok, cc, cnt)
            return g + 1, key, cnt
        return bit_step

    def bit_cond(c):
        return (c[0] < 16 // bits_per_check) & unsettled(c[1], c[2])

    def count_fine(cand):
        cf = _key_to_float(cand)
        return count_keys(lambda s, kb: s >= cf)

    _, key, cnt = lax.while_loop(bit_cond, make_step(0, lambda cand: count_coarse(_upper_half(_key_to_bits(cand)))),
                                 (jnp.int32(0), jnp.full((1, t), -2 ** 31, i32), qpos + 1))
    _, key, cnt = lax.while_loop(bit_cond, make_step(16 // bits_per_check, count_fine), (jnp.int32(0), key, cnt))
    thr = jnp.where(take_all, -FLT_MAX, _key_to_float(key))

    @pl.when(jnp.max(jnp.where(take_all, ktop, cnt)) > ktop)
    def _():
        def tally(kb, carry):
            gt, before = carry
            s = sc_ref[kb]
            bef_ref[kb] = before
            ties = jnp.sum(jnp.where(s == thr, 1.0, 0.0), axis=0, keepdims=True)
            return gt + jnp.where(s > thr, 1.0, 0.0), before + ties

        zero = jnp.zeros((1, t), f32)
        nquad = (qi + 4) // 4
        gt, _ = lax.fori_loop(0, 4 * nquad, tally, (jnp.zeros((t, t), f32), zero))
        need = ktop - jnp.sum(gt, axis=0, keepdims=True)
        tri = (qcol <= krow).astype(bf16)

        def drop(kb):
            s = sc_ref[kb]
            tie = s == thr
            run = bef_ref[kb] + jnp.dot(tri, jnp.where(tie, 1.0, 0.0).astype(bf16), preferred_element_type=f32)
            sc_ref[kb] = jnp.where(tie & (run > need), -jnp.inf, s)

        def drop_quad(kq, carry):
            for u in range(4):
                drop(4 * kq + u)
            return carry

        lax.fori_loop(0, nquad, drop_quad, 0)

    nq = sc_ref.shape[0]
    hw = ATT_HEADS * t
    acc_ref[...] = jnp.zeros(acc_ref.shape, f32)
    pt_ref[1] = jnp.zeros(pt_ref.shape[1:], bf16)
    al_ref[1] = jnp.ones(al_ref.shape[1:], f32)

    def logit_stage(j, par):
        for u in range(2):
            kb = jnp.minimum(2 * j + u, nq - 1)
            kblk = ak_ref[pl.ds(pl.multiple_of(kb * t, t), t), :]
            lg_ref[par, u] = lax.dot_general(kblk, qs_ref[...], NT_DIMS, preferred_element_type=f32)

    def softmax_stage(j, par, m_all, l_all):
        for u in range(2):
            kb = 2 * j + u
            madd = jnp.where(sc_ref[kb] >= thr, 0.0, NEG)
            near = jnp.clip(qi - kb, 0, 2)
            ms, ls, als = [], [], []
            for h in range(ATT_HEADS):
                hs = slice(t * h, t * (h + 1))
                lg = lg_ref[par, u, :, hs] + (madd + bias_ref[near, h])
                m_old = m_all[:, hs]
                m_new = jnp.maximum(m_old, jnp.max(lg, axis=0, keepdims=True))
                alpha = jnp.exp2(m_old - m_new)
                p = jnp.exp2(lg - m_new)
                ls.append(alpha * l_all[:, hs] + jnp.sum(p, axis=0, keepdims=True))
                ms.append(m_new)
                als.append(alpha)
                pt_ref[par, u, :, hs] = p.astype(bf16)
            al_ref[par, u] = jnp.concatenate(als, axis=1)
            m_all, l_all = jnp.concatenate(ms, axis=1), jnp.concatenate(ls, axis=1)
        return m_all, l_all

    def value_stage(j, par):
        for u in range(2):
            kb = jnp.maximum(2 * j + u, 0)
            pv = jnp.dot(avt_ref[kb], pt_ref[par, u], preferred_element_type=f32)
            acc_ref[...] = acc_ref[...] * al_ref[par, u] + pv

    def attn_trip(q, carry):
        for par in range(2):
            j = 2 * q + par
            value_stage(j - 1, 1 - par)
            carry = softmax_stage(j, par, *carry)
            logit_stage(j + 1, 1 - par)
        return carry

    def odd_tail(carry):
        j = 2 * (npair // 2)
        value_stage(j - 1, 1)
        carry = softmax_stage(j, 0, *carry)
        value_stage(j, 0)
        return carry

    def even_tail(carry):
        value_stage(2 * (npair // 2) - 1, 1)
        return carry

    logit_stage(0, 0)
    carry = (jnp.full((1, hw), NEG, f32), jnp.zeros((1, hw), f32))
    carry = lax.fori_loop(0, npair // 2, attn_trip, carry)
    _, l_all = lax.cond(npair % 2 == 1, odd_tail, even_tail, carry)

    out_t = acc_ref[...] / l_all
    out_t = jnp.concatenate([out_t[:, t * h:t * (h + 1)] for h in range(ATT_HEADS)], axis=0)
    o_ref[...] = out_t.T.astype(o_ref.dtype)


def _rel_bias_tiles(rel_bias):
    t = DSA_T
    ki = jnp.arange(t)[:, None]
    qj = jnp.arange(t)[None, :]
    far = rel_bias[REL_BUCKETS - 1].astype(f32)
    tiles = []
    for v in range(2):
        dist = jnp.maximum(v * t + qj - ki, 0)
        max_exact = REL_BUCKETS // 2
        nf = jnp.maximum(dist, 1).astype(f32)
        large = max_exact + (jnp.log(nf / max_exact) / math.log(REL_MAX_DIST / max_exact)
                             * (REL_BUCKETS - max_exact)).astype(i32)
        large = jnp.minimum(large, REL_BUCKETS - 1)
        bucket = jnp.where(dist < max_exact, dist, large)
        tiles.append(jnp.transpose((rel_bias[bucket].astype(f32) - far) * LOG2E, (2, 0, 1)))
    tiles.append(jnp.zeros_like(tiles[0]))
    return jnp.stack(tiles)


def _dsa(proj, iq, wt, ik, ak, avt, bias_tiles, batch, seq):
    n = proj.shape[0]
    t = DSA_T
    nq = seq // t
    assert nq % 4 == 0 and 2 * t >= REL_MAX_DIST
    ktop = min(TOPK_MAX, seq // 4)
    qrow = lambda width: pl.BlockSpec((t, width), lambda b, q: (b * nq + q, 0))
    kv = pl.BlockSpec((seq, 64), lambda b, q: (b, 0))
    hw = ATT_HEADS * t
    return pl.pallas_call(
        functools.partial(_dsa_kernel, ktop=ktop),
        out_shape=jax.ShapeDtypeStruct((n, ATT_W), bf16),
        grid=(batch, nq),
        in_specs=[qrow(256),
                  pl.BlockSpec((8, t), lambda b, q: (0, b * nq + q)),
                  pl.BlockSpec((t, COL_BLK), lambda b, q: (b * nq + q, CB_AQ)),
                  kv, kv,
                  pl.BlockSpec((nq, 64, t), lambda b, q: (b, 0, 0)),
                  pl.BlockSpec((3, ATT_HEADS, t, t), lambda b, q: (0, 0, 0, 0))],
        out_specs=qrow(ATT_W),
        scratch_shapes=[pltpu.VMEM((nq, t, t), f32),
                        pltpu.VMEM((nq, t, t), bf16),
                        pltpu.VMEM((IDX_HEADS * t, IDX_DH), bf16),
                        pltpu.VMEM((hw, ATT_DH), bf16),
                        pltpu.VMEM((2, 2, t, hw), f32),
                        pltpu.VMEM((2, 2, t, hw), bf16),
                        pltpu.VMEM((2, 2, 1, hw), f32),
                        pltpu.VMEM((ATT_DH, hw), f32),
                        pltpu.VMEM((nq, 1, t), f32)],
        compiler_params=_cparams(("parallel", "arbitrary")),
        name="dsa",
    )(iq, wt, proj, ik, ak, avt, bias_tiles)


def _merge_kernel(x_ref, ya_ref, yb_ref, ga_ref, gb_ref, wa_ref, wb_ref, wo_ref, g2_ref, wq_ref,
                  x1_ref, xnt_ref, qp_ref):
    ha = jnp.dot(ya_ref[...], wa_ref[...], preferred_element_type=f32)
    hb = jnp.dot(yb_ref[...], wb_ref[...], preferred_element_type=f32)
    h = jax.nn.sigmoid(ga_ref[...]) * ha + jax.nn.sigmoid(gb_ref[...]) * hb
    x1 = x_ref[...] + jnp.dot(h.astype(bf16), wo_ref[...], preferred_element_type=f32)
    x1_ref[...] = x1
    ms = jnp.mean(x1 * x1, axis=-1, keepdims=True)
    xn = x1 * lax.rsqrt(ms + EPS) * g2_ref[...]
    xnt_ref[...] = xn.T.astype(bf16)
    qp_ref[...] = jnp.dot(xn.astype(bf16), wq_ref[...], preferred_element_type=f32).astype(bf16)


def _merge(x2, ya, yb, proj, wa, wb, wo, g2, wq, tm=512):
    n = x2.shape[0]
    qw = wq.shape[1]
    row = lambda width: pl.BlockSpec((tm, width), lambda i: (i, 0))
    full = lambda a: pl.BlockSpec(a.shape, lambda i: (0, 0))
    return pl.pallas_call(
        _merge_kernel,
        out_shape=(jax.ShapeDtypeStruct((n, D_MODEL), f32),
                   jax.ShapeDtypeStruct((D_MODEL, n), bf16),
                   jax.ShapeDtypeStruct((n, qw), bf16)),
        grid=(n // tm,),
        in_specs=[row(D_MODEL), row(HG_W), row(ATT_W),
                  pl.BlockSpec((tm, D_MODEL), lambda i: (i, CB_GA)),
                  pl.BlockSpec((tm, D_MODEL), lambda i: (i, CB_GB)),
                  full(wa), full(wb), full(wo), full(g2), full(wq)],
        out_specs=(row(D_MODEL), pl.BlockSpec((D_MODEL, tm), lambda i: (0, i)), row(qw)),
        compiler_params=_cparams(("parallel",)),
        name="merge",
    )(x2, ya, yb, proj, proj, wa, wb, wo, g2, wq)


_PAIR_COUNTS = tuple(PEER_TOPK // (p + 1) for p in range(PEER_TOPK))
_NCAND = sum(_PAIR_COUNTS)
F32_SUBLANES = 8


def _exchange(v, i, j):
    v[i], v[j] = jnp.maximum(v[i], v[j]), jnp.minimum(v[i], v[j])


def _bitonic_sort(v):
    n = len(v)
    k = 2
    while k <= n:
        j = k // 2
        while j >= 1:
            for i in range(n):
                l = i ^ j
                if l > i:
                    if i & k:
                        _exchange(v, l, i)
                    else:
                        _exchange(v, i, l)
            j //= 2
        k *= 2


def _bitonic_merge(v):
    j = len(v) // 2
    while j >= 1:
        for i in range(len(v)):
            if i ^ j > i:
                _exchange(v, i, i ^ j)
        j //= 2


def _merge_sublanes(v, shifts):
    for shift in shifts:
        n = len(v)
        v = [jnp.maximum(v[g], pltpu.roll(v[n - 1 - g], shift, axis=0)) for g in range(n)]
        _bitonic_merge(v)
    return v


def _top_sorted(s):
    v = [s[F32_SUBLANES * g:F32_SUBLANES * (g + 1), :] for g in range(s.shape[0] // F32_SUBLANES)]
    _bitonic_sort(v)
    return _merge_sublanes(v[:PEER_TOPK], (4, 2, 1))


def _peer_route_kernel(qp_ref, k1_ref, k2_ref, n1_ref, a1_ref, r2_ref, e2_ref):
    tb = qp_ref.shape[0]
    sub = lax.broadcasted_iota(i32, (F32_SUBLANES, tb), 0)
    groups = PEER_NKEYS // F32_SUBLANES
    for h in range(PEER_HEADS):
        q1 = qp_ref[:, 2 * PEER_HALF * h:2 * PEER_HALF * h + PEER_HALF]
        q2 = qp_ref[:, 2 * PEER_HALF * h + PEER_HALF:2 * PEER_HALF * (h + 1)]
        s1 = lax.dot_general(k1_ref[h], q1, NT_DIMS, preferred_element_type=f32)
        s2 = lax.dot_general(k2_ref[h], q2, NT_DIMS, preferred_element_type=f32)
        a = _top_sorted(s1)
        b = _top_sorted(s2)
        nreg = pl.next_power_of_2(pl.cdiv(_NCAND, F32_SUBLANES))
        cand = [jnp.full((F32_SUBLANES, tb), -jnp.inf, f32) for _ in range(nreg)]
        r = 0
        for p in range(PEER_TOPK):
            for q in range(_PAIR_COUNTS[p]):
                cand[r // F32_SUBLANES] = jnp.where(sub == r % F32_SUBLANES, a[p] + b[q], cand[r // F32_SUBLANES])
                r += 1
        _bitonic_sort(cand)
        top = cand + [pltpu.roll(cand[len(cand) - 1 - g], 4, axis=0) for g in range(len(cand))]
        _bitonic_merge(top)
        top = _merge_sublanes(top, (2, 1))
        thr = top[PEER_TOPK - 1]
        mx = a[0] + b[0]
        z = jnp.zeros_like(mx)
        for tv in top:
            z = z + jnp.exp(tv - mx)
        s1g = s1.reshape(groups, F32_SUBLANES, tb)
        s2g = s2.reshape(groups, F32_SUBLANES, tb)
        n1 = jnp.zeros_like(s1g)
        r2 = jnp.zeros_like(s2g)
        for q in range(PEER_TOPK):
            n1 = n1 + jnp.where(s1g + b[q] >= thr, 1.0, 0.0)
            r2 = r2 + jnp.where(b[q] > s2g, 1.0, 0.0)
        n1_ref[h] = n1.reshape(PEER_NKEYS, tb)
        a1_ref[h] = (jnp.exp(s1g - a[0]) / z).reshape(PEER_NKEYS, tb)
        r2_ref[h] = r2.reshape(PEER_NKEYS, tb).astype(bf16)
        e2_ref[h] = jnp.exp(s2g - b[0]).reshape(PEER_NKEYS, tb).astype(bf16)


def _peer_route(qp, k1, k2, tb=512):
    n = qp.shape[0]
    big = lambda: pl.BlockSpec((PEER_HEADS, PEER_NKEYS, tb), lambda i: (0, 0, i))
    bigs = lambda dt: jax.ShapeDtypeStruct((PEER_HEADS, PEER_NKEYS, n), dt)
    return pl.pallas_call(
        _peer_route_kernel,
        out_shape=(bigs(f32), bigs(f32), bigs(bf16), bigs(bf16)),
        grid=(n // tb,),
        in_specs=[pl.BlockSpec((tb, qp.shape[1]), lambda i: (i, 0)),
                  pl.BlockSpec(k1.shape, lambda i: (0, 0, 0)),
                  pl.BlockSpec(k2.shape, lambda i: (0, 0, 0))],
        out_specs=(big(), big(), big(), big()),
        compiler_params=_cparams(("parallel",)),
        name="peer_route",
    )(qp, k1, k2)


PEER_EB = 512
PEER_NB = 4


def _peer_dense_kernel(xt_ref, u_ref, vt_ref, n1_ref, a1_ref, r2_ref, e2_ref, x1_ref, gf_ref,
                       o_ref, acc_ref, act_ref, g_ref, *, final_norm):
    e = pl.program_id(1)
    tb = xt_ref.shape[1]

    @pl.when(e == 0)
    def _():
        acc_ref[...] = jnp.zeros_like(acc_ref)

    nblk = PEER_EB // PEER_NKEYS
    reps = PEER_NKEYS // BF16_SUBLANES

    def gates(blk, slot):
        for ii in range(nblk):
            i = jnp.minimum(blk * nblk + ii, PEER_NKEYS - 1)
            g = jnp.zeros((PEER_NKEYS, tb), bf16)
            for h in range(PEER_HEADS):
                n_row = jnp.broadcast_to(n1_ref[h, pl.ds(i, 1), :], (BF16_SUBLANES, tb)).astype(bf16)
                a_row = jnp.broadcast_to(a1_ref[h, pl.ds(i, 1), :], (BF16_SUBLANES, tb)).astype(bf16)
                n_all = jnp.tile(n_row, (reps, 1))
                a_all = jnp.tile(a_row, (reps, 1))
                g = g + jnp.where(r2_ref[h] < n_all, e2_ref[h] * a_all, jnp.zeros((), bf16))
            g_ref[slot, PEER_NKEYS * ii:PEER_NKEYS * (ii + 1), :] = g

    def hidden(slot):
        return jnp.dot(u_ref[PEER_EB * slot:PEER_EB * (slot + 1), :], xt_ref[...], preferred_element_type=f32)

    def outputs(slot, hh):
        act = 0.5 * hh * (1.0 + lax.erf(hh * (2.0 ** -0.5)))
        act_ref[slot] = (act * g_ref[slot].astype(f32)).astype(bf16)
        acc_ref[...] += jnp.dot(vt_ref[:, PEER_EB * slot:PEER_EB * (slot + 1)], act_ref[slot],
                                preferred_element_type=f32)

    @pl.when(e == 0)
    def _():
        for slot in range(1, PEER_NB):
            gates(slot, slot)

    gates(PEER_NB * e, 0)
    hts = [hidden(slot) for slot in range(PEER_NB)]
    for slot in range(PEER_NB):
        outputs(slot, hts[slot])
    for slot in range(1, PEER_NB):
        gates(PEER_NB * (e + 1) + slot, slot)

    @pl.when(e == pl.num_programs(1) - 1)
    def _():
        y = x1_ref[...] + acc_ref[...].T
        if final_norm:
            ms = jnp.mean(y * y, axis=-1, keepdims=True)
            y = y * lax.rsqrt(ms + EPS) * gf_ref[...]
        o_ref[...] = y


def _peer_dense(xt, u, vt, n1, a1, r2, e2, x1, gf, final_norm, tb=512):
    n = xt.shape[1]
    ne = u.shape[0]
    step = PEER_NB * PEER_EB
    big = lambda: pl.BlockSpec((PEER_HEADS, PEER_NKEYS, tb), lambda t, e: (0, 0, t))
    return pl.pallas_call(
        functools.partial(_peer_dense_kernel, final_norm=final_norm),
        out_shape=jax.ShapeDtypeStruct((n, D_MODEL), f32),
        grid=(n // tb, ne // step),
        in_specs=[pl.BlockSpec((D_MODEL, tb), lambda t, e: (0, t)),
                  pl.BlockSpec((step, D_MODEL), lambda t, e: (e, 0)),
                  pl.BlockSpec((D_MODEL, step), lambda t, e: (0, e)),
                  big(), big(), big(), big(),
                  pl.BlockSpec((tb, D_MODEL), lambda t, e: (t, 0)),
                  pl.BlockSpec((1, D_MODEL), lambda t, e: (0, 0))],
        out_specs=pl.BlockSpec((tb, D_MODEL), lambda t, e: (t, 0)),
        scratch_shapes=[pltpu.VMEM((D_MODEL, tb), f32), pltpu.VMEM((PEER_NB, PEER_EB, tb), bf16),
                        pltpu.VMEM((PEER_NB, PEER_EB, tb), bf16)],
        compiler_params=_cparams(("parallel", "arbitrary")),
        name="peer_dense",
    )(xt, u, vt, n1, a1, r2, e2, x1, gf)


def _pack_w_in(w):
    offs = np.concatenate([[0], np.cumsum(np.array(IN_SIZES))]).tolist()
    seg = lambda k: w[:, offs[k]:offs[k + 1]]
    hq, hf, hi, hog, aq, ak, av, iq, ik, iw, ga, gb = [seg(k) for k in range(12)]
    pad = jnp.zeros((w.shape[0], 64 - IDX_HEADS), w.dtype)
    return jnp.concatenate([hq, hf, hi, hog, aq, iq, ak, av, ik, iw, pad, ga, gb], axis=1).astype(bf16)


def _layer(x2, batch, seq, norm_mix, w_in, lb, hg_norm, ikg, ikb, bias_tiles, w_up_a, w_up_b, w_out,
           norm_ffn, peer_wq, peer_keys, peer_u, peer_v):
    proj = _inproj(x2, norm_mix.reshape(1, -1), _pack_w_in(w_in))
    iq, ak, ik, avt, wt = _dsa_prep(proj, ikg.reshape(1, -1), ikb.reshape(1, -1))
    ya = _hgrn2(proj, lb.reshape(1, -1), hg_norm.reshape(1, -1), batch, seq)
    yb = _dsa(proj, iq, wt, ik, ak, avt, bias_tiles, batch, seq)
    x1, xn, qp = _merge(x2, ya, yb, proj, w_up_a.astype(bf16), w_up_b.astype(bf16), w_out.astype(bf16),
                        norm_ffn.reshape(1, -1), peer_wq.astype(bf16))
    route = _peer_route(qp, peer_keys[0].astype(bf16), peer_keys[1].astype(bf16))
    return xn, route, x1


def kernel(x, norm_mix, w_in, hg_lb, hg_norm, idx_k_norm_g, idx_k_norm_b, rel_bias, w_up_a, w_up_b, w_out,
           norm_ffn, peer_wq, peer_keys, peer_u, peer_v, norm_final):
    batch, seq, _ = x.shape
    depth = w_in.shape[0]
    lb_all = jnp.cumsum(jax.nn.softmax(hg_lb.astype(f32), axis=0), axis=0)
    bias_tiles = _rel_bias_tiles(rel_bias)
    x2 = x.reshape(batch * seq, D_MODEL)
    gf = norm_final.reshape(1, -1)
    for l in range(depth):
        xn, route, x1 = _layer(
            x2, batch, seq, norm_mix[l], w_in[l], lb_all[l], hg_norm[l], idx_k_norm_g[l], idx_k_norm_b[l],
            bias_tiles, w_up_a[l], w_up_b[l], w_out[l], norm_ffn[l], peer_wq[l], peer_keys[l], peer_u[l], peer_v[l])
        x2 = _peer_dense(xn, peer_u[l].astype(bf16), peer_v[l].astype(bf16).T, *route, x1, gf,
                         final_norm=(l == depth - 1))
    return x2.reshape(batch, seq, D_MODEL)
```

```python
import functools
import math

import jax
import jax.numpy as jnp
import numpy as np
from jax import lax
from jax.experimental import pallas as pl
from jax.experimental.pallas import tpu as pltpu

f32 = jnp.float32
bf16 = jnp.bfloat16
i32 = jnp.int32

D_MODEL = 1024
EPS = 1e-6
HG_HEADS, HG_D = 4, 128
HG_W = HG_HEADS * HG_D
ATT_HEADS, ATT_DH = 8, 64
ATT_W = ATT_HEADS * ATT_DH
IDX_HEADS, IDX_DH = 4, 64
TOPK_MAX = 256
REL_BUCKETS, REL_MAX_DIST = 32, 128
PEER_HEADS, PEER_NKEYS, PEER_HALF, PEER_TOPK = 8, 128, 128, 16
PEER_EXPERTS = PEER_NKEYS * PEER_NKEYS
IN_SIZES = (HG_W, HG_W, HG_W, HG_W, ATT_W, ATT_DH, ATT_DH, IDX_HEADS * IDX_DH, IDX_DH, IDX_HEADS, D_MODEL, D_MODEL)

LANES = 128
BF16_SUBLANES = 16
VMEM_LIMIT = 56 * 1024 * 1024

COL_BLK = 512
PROJ_W = 10 * COL_BLK
CB_HQ, CB_HF, CB_HI, CB_HOG, CB_AQ, CB_IDX = 0, 1, 2, 3, 4, 5
CB_GA, CB_GB = 3, 4
NEG = -1e30


def _cparams(sem):
    return pltpu.CompilerParams(dimension_semantics=sem, vmem_limit_bytes=VMEM_LIMIT)


def _inproj_kernel(x_ref, g_ref, w_ref, o_ref, xn_ref):
    @pl.when(pl.program_id(1) == 0)
    def _():
        x = x_ref[...]
        ms = jnp.mean(x * x, axis=-1, keepdims=True)
        xn_ref[...] = (x * lax.rsqrt(ms + EPS) * g_ref[...]).astype(bf16)

    o_ref[...] = jnp.dot(xn_ref[...], w_ref[...], preferred_element_type=f32)


def _inproj(x2, g, w, tm=1024):
    n = x2.shape[0]
    return pl.pallas_call(
        _inproj_kernel,
        out_shape=jax.ShapeDtypeStruct((n, PROJ_W), f32),
        grid=(n // tm, PROJ_W // COL_BLK),
        in_specs=[
            pl.BlockSpec((tm, D_MODEL), lambda i, j: (i, 0)),
            pl.BlockSpec((1, D_MODEL), lambda i, j: (0, 0)),
            pl.BlockSpec((D_MODEL, COL_BLK), lambda i, j: (0, j)),
        ],
        out_specs=pl.BlockSpec((tm, COL_BLK), lambda i, j: (i, j)),
        scratch_shapes=[pltpu.VMEM((tm, D_MODEL), bf16)],
        compiler_params=_cparams(("parallel", "arbitrary")),
        name="inproj",
    )(x2, g, w)


DSA_T = 128
DSA_VROWS = ATT_DH + BF16_SUBLANES


def _dsa_prep_kernel(p_ref, g_ref, b_ref, iq_ref, ak_ref, ik_ref, avt_ref, wt_ref):
    t = DSA_T
    p = p_ref[...]
    iq_ref[...] = p[:, 0:256].astype(bf16)
    ak_ref[...] = p[:, 256:320].astype(bf16)
    ik = p[:, 384:448]
    mu = jnp.mean(ik, axis=-1, keepdims=True)
    var = jnp.mean(jnp.square(ik - mu), axis=-1, keepdims=True)
    y = (ik - mu) * lax.rsqrt(var + EPS) * g_ref[...] + b_ref[...]
    ik_ref[...] = y.astype(bf16)
    kv_t = p[:, 256:384].T
    extra = jnp.where(lax.broadcasted_iota(i32, (DSA_VROWS - ATT_DH, t), 0) == 0, 1.0, 0.0).astype(bf16)
    for c in range(p.shape[0] // t):
        avt_ref[c] = jnp.concatenate([kv_t[64:128, t * c:t * (c + 1)].astype(bf16), extra], axis=0)
    wt_ref[...] = p[:, 384:512].T[64:72, :] * (IDX_HEADS ** -0.5 * IDX_DH ** -0.5)


def _dsa_prep(proj, g, b, tm=1024):
    n = proj.shape[0]
    t = DSA_T
    row = lambda w: pl.BlockSpec((tm, w), lambda i: (i, 0))
    return pl.pallas_call(
        _dsa_prep_kernel,
        out_shape=(
            jax.ShapeDtypeStruct((n, 256), bf16),
            jax.ShapeDtypeStruct((n, 64), bf16),
            jax.ShapeDtypeStruct((n, 64), bf16),
            jax.ShapeDtypeStruct((n // t, DSA_VROWS, t), bf16),
            jax.ShapeDtypeStruct((8, n), f32),
        ),
        grid=(n // tm,),
        in_specs=[
            pl.BlockSpec((tm, COL_BLK), lambda i: (i, CB_IDX)),
            pl.BlockSpec((1, 64), lambda i: (0, 0)),
            pl.BlockSpec((1, 64), lambda i: (0, 0)),
        ],
        out_specs=(row(256), row(64), row(64),
                   pl.BlockSpec((tm // t, DSA_VROWS, t), lambda i: (i, 0, 0)),
                   pl.BlockSpec((8, tm), lambda i: (0, i))),
        compiler_params=_cparams(("parallel",)),
        name="dsa_prep",
    )(proj, g, b)


HG_CHUNK = 32


def _split3(a):
    a1 = a.astype(bf16)
    r1 = a - a1.astype(f32)
    a2 = r1.astype(bf16)
    r2 = r1 - a2.astype(f32)
    return a1, a2, r2.astype(bf16)


def _hgrn2_kernel(q_ref, f_ref, i_ref, og_ref, lb_ref, gn_ref, o_ref, st_ref, oacc_ref, *, ts):
    c = HG_CHUNK

    @pl.when(pl.program_id(1) == 0)
    def _():
        st_ref[...] = jnp.zeros_like(st_ref)

    r_io = lax.broadcasted_iota(i32, (c, c), 0)
    c_io = lax.broadcasted_iota(i32, (c, c), 1)
    causal = c_io <= r_io
    tri = causal.astype(bf16)
    mid = c // 2 - 1

    heads = range(HG_HEADS)
    lanes = [slice(HG_D * h, HG_D * (h + 1)) for h in heads]

    def chunk_pair(cp, carry):
        chains = [(pl.ds(pl.multiple_of((2 * cp + u) * c, c), c), h) for u in range(2) for h in heads]
        q, k, v, b = [], [], [], []
        for rows, h in chains:
            qr = q_ref[rows, lanes[h]]
            q.append(qr * jax.nn.sigmoid(qr))
            lb = lb_ref[:, lanes[h]]
            fg = lb + (1.0 - lb) * jax.nn.sigmoid(f_ref[rows, lanes[h]])
            k.append(1.0 - fg)
            v.append(i_ref[rows, lanes[h]].astype(bf16))
            l1, l2, l3 = _split3(jnp.log(fg))
            b.append(jnp.dot(tri, l1, preferred_element_type=f32)
                     + jnp.dot(tri, l2, preferred_element_type=f32)
                     + jnp.dot(tri, l3, preferred_element_type=f32))
        a = []
        for n in range(len(chains)):
            b_mid = b[n][mid:mid + 1, :]
            qt = (q[n] * jnp.exp(b[n] - b_mid)).astype(bf16)
            kt = (k[n] * jnp.exp(b_mid - b[n])).astype(bf16)
            a.append(lax.dot_general(qt, kt, NT_DIMS, preferred_element_type=f32))
        intra = [jnp.dot(jnp.where(causal, a[n], 0.0).astype(bf16), v[n], preferred_element_type=f32)
                 for n in range(len(chains))]
        for n, (rows, h) in enumerate(chains):
            b_last = b[n][c - 1:c, :]
            st = st_ref[h]
            qe = (q[n] * jnp.exp(b[n])).astype(bf16)
            inter = lax.dot_general(qe, st.astype(bf16), NT_DIMS, preferred_element_type=f32)
            oacc_ref[rows, lanes[h]] = inter + intra[n]
            ke = (k[n] * jnp.exp(b_last - b[n])).astype(bf16)
            upd = lax.dot_general(v[n], ke, (((0,), (0,)), ((), ())), preferred_element_type=f32)
            st_ref[h] = st * jnp.exp(b_last) + upd
        return carry

    lax.fori_loop(0, ts // (2 * c), chunk_pair, 0)

    for h in range(HG_HEADS):
        sl = slice(HG_D * h, HG_D * (h + 1))
        o = oacc_ref[:, sl]
        ms = jnp.mean(o * o, axis=-1, keepdims=True)
        og = og_ref[:, sl]
        y = o * lax.rsqrt(ms + EPS) * gn_ref[:, sl] * (og * jax.nn.sigmoid(og))
        o_ref[:, sl] = y.astype(o_ref.dtype)


def _hgrn2(proj, lb, gn, batch, seq, ts=256):
    n = proj.shape[0]
    nsb = seq // ts
    col = lambda cb: pl.BlockSpec((ts, COL_BLK), lambda b, s, cb=cb: (b * nsb + s, cb))
    return pl.pallas_call(
        functools.partial(_hgrn2_kernel, ts=ts),
        out_shape=jax.ShapeDtypeStruct((n, HG_W), bf16),
        grid=(batch, nsb),
        in_specs=[col(CB_HQ), col(CB_HF), col(CB_HI), col(CB_HOG),
                  pl.BlockSpec((1, HG_W), lambda b, s: (0, 0)),
                  pl.BlockSpec((1, HG_W), lambda b, s: (0, 0))],
        out_specs=pl.BlockSpec((ts, HG_W), lambda b, s: (b * nsb + s, 0)),
        scratch_shapes=[pltpu.VMEM((HG_HEADS, HG_D, HG_D), f32), pltpu.VMEM((ts, HG_W), f32)],
        compiler_params=_cparams(("parallel", "arbitrary")),
        name="hgrn2",
    )(proj, proj, proj, proj, lb, gn)


LOG2E = 1.4426950408889634
FLT_MAX = 3.4028234663852886e38
NT_DIMS = (((1,), (1,)), ((), ()))


def _key_to_float(key):
    bits = jnp.where(key >= 0, key, key ^ jnp.int32(0x7FFFFFFF))
    return pltpu.bitcast(bits, f32)


def _dsa_kernel(iq_ref, wt_ref, aq_ref, ik_ref, ak_ref, avt_ref, bias_ref, o_ref,
                sc_ref, iqs_ref, qs_ref, lg_ref, pt_ref, al_ref, acc_ref, bef_ref, *, ktop):
    t = DSA_T
    qi = pl.program_id(1)
    npair = (qi + 2) // 2
    krow = lax.broadcasted_iota(i32, (t, t), 0)
    qcol = lax.broadcasted_iota(i32, (t, t), 1)
    qpos = qi * t + lax.broadcasted_iota(i32, (1, t), 1)

    iq = iq_ref[...]
    for j in range(IDX_HEADS):
        iqs_ref[t * j:t * (j + 1), :] = iq[:, IDX_DH * j:IDX_DH * (j + 1)]
    aq = aq_ref[...] * (ATT_DH ** -0.5 * LOG2E)
    for h in range(ATT_HEADS):
        qs_ref[t * h:t * (h + 1), :] = aq[:, ATT_DH * h:ATT_DH * (h + 1)].astype(bf16)
    wt = wt_ref[...]

    def score_pair(kp):
        ksl = pl.ds(pl.multiple_of(kp * 2 * t, 2 * t), 2 * t)
        rel = lax.dot_general(ik_ref[ksl, :], iqs_ref[...], NT_DIMS, preferred_element_type=f32)
        for u in range(2):
            kb = 2 * kp + u
            s = jnp.zeros((t, t), f32)
            for j in range(IDX_HEADS):
                s = s + wt[j:j + 1, :] * jnp.maximum(rel[t * u:t * (u + 1), t * j:t * (j + 1)], 0.0)
            sc_ref[kb] = jnp.where((kb * t + krow) <= (qi * t + qcol), s, -jnp.inf)

    def score_quad(kq, carry):
        score_pair(2 * kq)
        score_pair(2 * kq + 1)
        return carry

    lax.fori_loop(0, (qi + 4) // 4, score_quad, 0)

    def count_keys(pred):
        def body(kp, acc):
            acc = acc + pred(sc_ref[2 * kp], 2 * kp).astype(i32)
            return acc + pred(sc_ref[2 * kp + 1], 2 * kp + 1).astype(i32)
        acc = lax.fori_loop(0, npair, body, jnp.zeros((t, t), i32))
        return jnp.sum(acc, axis=0, keepdims=True)

    take_all = qpos < ktop
    few_pos = count_keys(lambda s, kb: s > 0.0) < ktop

    def unsettled(key, cnt):
        done = take_all | (cnt == ktop) | (few_pos & (key == 0))
        return jnp.min(done.astype(i32)) == 0

    bits_per_check = 4

    def bit_cond(c):
        return (c[0] < 32 // bits_per_check) & unsettled(c[1], c[2])

    def bit_step(c):
        g, key, cnt = c
        for b in range(bits_per_check):
            cand = key + jnp.left_shift(jnp.int32(1), 31 - b - bits_per_check * g)
            cf = _key_to_float(cand)
            cc = count_keys(lambda s, kb: s >= cf)
            ok = cc >= ktop
            key = jnp.where(ok, cand, key)
            cnt = jnp.where(ok, cc, cnt)
        return g + 1, key, cnt

    _, key, cnt = lax.while_loop(bit_cond, bit_step,
                                 (jnp.int32(0), jnp.full((1, t), -2 ** 31, i32), qpos + 1))
    thr = jnp.where(take_all, -FLT_MAX, _key_to_float(key))

    @pl.when(jnp.max(jnp.where(take_all, ktop, cnt)) > ktop)
    def _():
        def tally(kb, carry):
            gt, before = carry
            s = sc_ref[kb]
            bef_ref[kb] = before
            ties = jnp.sum(jnp.where(s == thr, 1.0, 0.0), axis=0, keepdims=True)
            return gt + jnp.where(s > thr, 1.0, 0.0), before + ties

        zero = jnp.zeros((1, t), f32)
        nquad = (qi + 4) // 4
        gt, _ = lax.fori_loop(0, 4 * nquad, tally, (jnp.zeros((t, t), f32), zero))
        need = ktop - jnp.sum(gt, axis=0, keepdims=True)
        tri = (qcol <= krow).astype(bf16)

        def drop(kb):
            s = sc_ref[kb]
            tie = s == thr
            run = bef_ref[kb] + jnp.dot(tri, jnp.where(tie, 1.0, 0.0).astype(bf16), preferred_element_type=f32)
            sc_ref[kb] = jnp.where(tie & (run > need), -jnp.inf, s)

        def drop_quad(kq, carry):
            for u in range(4):
                drop(4 * kq + u)
            return carry

        lax.fori_loop(0, nquad, drop_quad, 0)

    nq = sc_ref.shape[0]
    hw = ATT_HEADS * t
    acc_ref[...] = jnp.zeros(acc_ref.shape, f32)
    pt_ref[1] = jnp.zeros(pt_ref.shape[1:], bf16)
    al_ref[1] = jnp.ones(al_ref.shape[1:], f32)

    def logit_stage(j, par):
        for u in range(2):
            kb = jnp.minimum(2 * j + u, nq - 1)
            kblk = ak_ref[pl.ds(pl.multiple_of(kb * t, t), t), :]
            lg_ref[par, u] = lax.dot_general(kblk, qs_ref[...], NT_DIMS, preferred_element_type=f32)

    def softmax_stage(j, par, m_all):
        for u in range(2):
            kb = 2 * j + u
            madd = jnp.where(sc_ref[kb] >= thr, 0.0, NEG)
            near = jnp.clip(qi - kb, 0, 2)
            ms, als = [], []
            for h in range(ATT_HEADS):
                hs = slice(t * h, t * (h + 1))
                lg = lg_ref[par, u, :, hs] + (madd + bias_ref[near, h])
                m_old = m_all[:, hs]
                m_new = jnp.maximum(m_old, jnp.max(lg, axis=0, keepdims=True))
                als.append(jnp.exp2(m_old - m_new))
                ms.append(m_new)
                pt_ref[par, u, :, hs] = jnp.exp2(lg - m_new).astype(bf16)
            al_ref[par, u] = jnp.concatenate(als, axis=1)
            m_all = jnp.concatenate(ms, axis=1)
        return m_all

    def value_stage(j, par):
        for u in range(2):
            kb = jnp.maximum(2 * j + u, 0)
            pv = jnp.dot(avt_ref[kb], pt_ref[par, u], preferred_element_type=f32)
            acc_ref[...] = acc_ref[...] * al_ref[par, u] + pv

    def attn_trip(q, carry):
        for par in range(2):
            j = 2 * q + par
            value_stage(j - 1, 1 - par)
            carry = softmax_stage(j, par, carry)
            logit_stage(j + 1, 1 - par)
        return carry

    logit_stage(0, 0)
    m_all = lax.fori_loop(0, npair // 2, attn_trip, jnp.full((1, hw), NEG, f32))

    @pl.when(npair % 2 == 1)
    def _():
        j = 2 * (npair // 2)
        value_stage(j - 1, 1)
        softmax_stage(j, 0, m_all)
        value_stage(j, 0)

    @pl.when(npair % 2 == 0)
    def _():
        value_stage(2 * (npair // 2) - 1, 1)

    acc = acc_ref[...]
    out_t = acc[:ATT_DH] / acc[ATT_DH:ATT_DH + 1]
    out_t = jnp.concatenate([out_t[:, t * h:t * (h + 1)] for h in range(ATT_HEADS)], axis=0)
    o_ref[...] = out_t.T.astype(o_ref.dtype)


def _rel_bias_tiles(rel_bias):
    t = DSA_T
    ki = jnp.arange(t)[:, None]
    qj = jnp.arange(t)[None, :]
    far = rel_bias[REL_BUCKETS - 1].astype(f32)
    tiles = []
    for v in range(2):
        dist = jnp.maximum(v * t + qj - ki, 0)
        max_exact = REL_BUCKETS // 2
        nf = jnp.maximum(dist, 1).astype(f32)
        large = max_exact + (jnp.log(nf / max_exact) / math.log(REL_MAX_DIST / max_exact)
                             * (REL_BUCKETS - max_exact)).astype(i32)
        large = jnp.minimum(large, REL_BUCKETS - 1)
        bucket = jnp.where(dist < max_exact, dist, large)
        tiles.append(jnp.transpose((rel_bias[bucket].astype(f32) - far) * LOG2E, (2, 0, 1)))
    tiles.append(jnp.zeros_like(tiles[0]))
    return jnp.stack(tiles)


def _dsa(proj, iq, wt, ik, ak, avt, bias_tiles, batch, seq):
    n = proj.shape[0]
    t = DSA_T
    nq = seq // t
    assert nq % 4 == 0 and 2 * t >= REL_MAX_DIST
    ktop = min(TOPK_MAX, seq // 4)
    qrow = lambda width: pl.BlockSpec((t, width), lambda b, q: (b * nq + q, 0))
    kv = pl.BlockSpec((seq, 64), lambda b, q: (b, 0))
    hw = ATT_HEADS * t
    return pl.pallas_call(
        functools.partial(_dsa_kernel, ktop=ktop),
        out_shape=jax.ShapeDtypeStruct((n, ATT_W), bf16),
        grid=(batch, nq),
        in_specs=[qrow(256),
                  pl.BlockSpec((8, t), lambda b, q: (0, b * nq + q)),
                  pl.BlockSpec((t, COL_BLK), lambda b, q: (b * nq + q, CB_AQ)),
                  kv, kv,
                  pl.BlockSpec((nq, DSA_VROWS, t), lambda b, q: (b, 0, 0)),
                  pl.BlockSpec((3, ATT_HEADS, t, t), lambda b, q: (0, 0, 0, 0))],
        out_specs=qrow(ATT_W),
        scratch_shapes=[pltpu.VMEM((nq, t, t), f32),
                        pltpu.VMEM((IDX_HEADS * t, IDX_DH), bf16),
                        pltpu.VMEM((hw, ATT_DH), bf16),
                        pltpu.VMEM((2, 2, t, hw), f32),
                        pltpu.VMEM((2, 2, t, hw), bf16),
                        pltpu.VMEM((2, 2, 1, hw), f32),
                        pltpu.VMEM((DSA_VROWS, hw), f32),
                        pltpu.VMEM((nq, 1, t), f32)],
        compiler_params=_cparams(("parallel", "arbitrary")),
        name="dsa",
    )(iq, wt, proj, ik, ak, avt, bias_tiles)


def _merge_kernel(x_ref, ya_ref, yb_ref, ga_ref, gb_ref, wa_ref, wb_ref, wo_ref, g2_ref, wq_ref,
                  x1_ref, xnt_ref, qp_ref):
    ha = jnp.dot(ya_ref[...], wa_ref[...], preferred_element_type=f32)
    hb = jnp.dot(yb_ref[...], wb_ref[...], preferred_element_type=f32)
    h = jax.nn.sigmoid(ga_ref[...]) * ha + jax.nn.sigmoid(gb_ref[...]) * hb
    x1 = x_ref[...] + jnp.dot(h.astype(bf16), wo_ref[...], preferred_element_type=f32)
    x1_ref[...] = x1
    ms = jnp.mean(x1 * x1, axis=-1, keepdims=True)
    xn = x1 * lax.rsqrt(ms + EPS) * g2_ref[...]
    xnt_ref[...] = xn.T.astype(bf16)
    qp_ref[...] = jnp.dot(xn.astype(bf16), wq_ref[...], preferred_element_type=f32).astype(bf16)


def _merge(x2, ya, yb, proj, wa, wb, wo, g2, wq, tm=512):
    n = x2.shape[0]
    qw = wq.shape[1]
    row = lambda width: pl.BlockSpec((tm, width), lambda i: (i, 0))
    full = lambda a: pl.BlockSpec(a.shape, lambda i: (0, 0))
    return pl.pallas_call(
        _merge_kernel,
        out_shape=(jax.ShapeDtypeStruct((n, D_MODEL), f32),
                   jax.ShapeDtypeStruct((D_MODEL, n), bf16),
                   jax.ShapeDtypeStruct((n, qw), bf16)),
        grid=(n // tm,),
        in_specs=[row(D_MODEL), row(HG_W), row(ATT_W),
                  pl.BlockSpec((tm, D_MODEL), lambda i: (i, CB_GA)),
                  pl.BlockSpec((tm, D_MODEL), lambda i: (i, CB_GB)),
                  full(wa), full(wb), full(wo), full(g2), full(wq)],
        out_specs=(row(D_MODEL), pl.BlockSpec((D_MODEL, tm), lambda i: (0, i)), row(qw)),
        compiler_params=_cparams(("parallel",)),
        name="merge",
    )(x2, ya, yb, proj, proj, wa, wb, wo, g2, wq)


_PAIR_COUNTS = tuple(PEER_TOPK // (p + 1) for p in range(PEER_TOPK))
_NCAND = sum(_PAIR_COUNTS)
F32_SUBLANES = 8


def _exchange(v, i, j):
    v[i], v[j] = jnp.maximum(v[i], v[j]), jnp.minimum(v[i], v[j])


def _bitonic_sort(v):
    n = len(v)
    k = 2
    while k <= n:
        j = k // 2
        while j >= 1:
            for i in range(n):
                l = i ^ j
                if l > i:
                    if i & k:
                        _exchange(v, l, i)
                    else:
                        _exchange(v, i, l)
            j //= 2
        k *= 2


def _bitonic_merge(v):
    j = len(v) // 2
    while j >= 1:
        for i in range(len(v)):
            if i ^ j > i:
                _exchange(v, i, i ^ j)
        j //= 2


def _merge_sublanes(v, shifts):
    for shift in shifts:
        n = len(v)
        v = [jnp.maximum(v[g], pltpu.roll(v[n - 1 - g], shift, axis=0)) for g in range(n)]
        _bitonic_merge(v)
    return v


def _top_sorted(s):
    v = [s[F32_SUBLANES * g:F32_SUBLANES * (g + 1), :] for g in range(s.shape[0] // F32_SUBLANES)]
    _bitonic_sort(v)
    return _merge_sublanes(v[:PEER_TOPK], (4, 2, 1))


def _peer_route_kernel(qp_ref, k1_ref, k2_ref, n1_ref, a1_ref, r2_ref, e2_ref):
    tb = qp_ref.shape[0]
    sub = lax.broadcasted_iota(i32, (F32_SUBLANES, tb), 0)
    groups = PEER_NKEYS // F32_SUBLANES
    for h in range(PEER_HEADS):
        q1 = qp_ref[:, 2 * PEER_HALF * h:2 * PEER_HALF * h + PEER_HALF]
        q2 = qp_ref[:, 2 * PEER_HALF * h + PEER_HALF:2 * PEER_HALF * (h + 1)]
        s1 = lax.dot_general(k1_ref[h], q1, NT_DIMS, preferred_element_type=f32)
        s2 = lax.dot_general(k2_ref[h], q2, NT_DIMS, preferred_element_type=f32)
        a = _top_sorted(s1)
        b = _top_sorted(s2)
        nreg = pl.next_power_of_2(pl.cdiv(_NCAND, F32_SUBLANES))
        cand = [jnp.full((F32_SUBLANES, tb), -jnp.inf, f32) for _ in range(nreg)]
        r = 0
        for p in range(PEER_TOPK):
            for q in range(_PAIR_COUNTS[p]):
                cand[r // F32_SUBLANES] = jnp.where(sub == r % F32_SUBLANES, a[p] + b[q], cand[r // F32_SUBLANES])
                r += 1
        _bitonic_sort(cand)
        top = cand + [pltpu.roll(cand[len(cand) - 1 - g], 4, axis=0) for g in range(len(cand))]
        _bitonic_merge(top)
        top = _merge_sublanes(top, (2, 1))
        thr = top[PEER_TOPK - 1]
        mx = a[0] + b[0]
        z = jnp.zeros_like(mx)
        for tv in top:
            z = z + jnp.exp(tv - mx)
        s1g = s1.reshape(groups, F32_SUBLANES, tb)
        s2g = s2.reshape(groups, F32_SUBLANES, tb)
        n1 = jnp.zeros_like(s1g)
        r2 = jnp.zeros_like(s2g)
        for q in range(PEER_TOPK):
            n1 = n1 + jnp.where(s1g + b[q] >= thr, 1.0, 0.0)
            r2 = r2 + jnp.where(b[q] > s2g, 1.0, 0.0)
        n1_ref[h] = n1.reshape(PEER_NKEYS, tb)
        a1_ref[h] = (jnp.exp(s1g - a[0]) / z).reshape(PEER_NKEYS, tb)
        r2_ref[h] = r2.reshape(PEER_NKEYS, tb).astype(bf16)
        e2_ref[h] = jnp.exp(s2g - b[0]).reshape(PEER_NKEYS, tb).astype(bf16)


def _peer_route(qp, k1, k2, tb=256):
    n = qp.shape[0]
    big = lambda: pl.BlockSpec((PEER_HEADS, PEER_NKEYS, tb), lambda i: (0, 0, i))
    bigs = lambda dt: jax.ShapeDtypeStruct((PEER_HEADS, PEER_NKEYS, n), dt)
    return pl.pallas_call(
        _peer_route_kernel,
        out_shape=(bigs(f32), bigs(f32), bigs(bf16), bigs(bf16)),
        grid=(n // tb,),
        in_specs=[pl.BlockSpec((tb, qp.shape[1]), lambda i: (i, 0)),
                  pl.BlockSpec(k1.shape, lambda i: (0, 0, 0)),
                  pl.BlockSpec(k2.shape, lambda i: (0, 0, 0))],
        out_specs=(big(), big(), big(), big()),
        compiler_params=_cparams(("parallel",)),
        name="peer_route",
    )(qp, k1, k2)


PEER_EB = 512
PEER_NB = 4


def _peer_dense_kernel(xt_ref, u_ref, vt_ref, n1_ref, a1_ref, r2_ref, e2_ref, x1_ref, gf_ref,
                       o_ref, acc_ref, act_ref, g_ref, *, final_norm):
    e = pl.program_id(1)
    tb = xt_ref.shape[1]

    @pl.when(e == 0)
    def _():
        acc_ref[...] = jnp.zeros_like(acc_ref)

    nblk = PEER_EB // PEER_NKEYS
    reps = PEER_NKEYS // BF16_SUBLANES

    def gates(blk, slot):
        for ii in range(nblk):
            i = jnp.minimum(blk * nblk + ii, PEER_NKEYS - 1)
            g = jnp.zeros((PEER_NKEYS, tb), bf16)
            for h in range(PEER_HEADS):
                n_row = jnp.broadcast_to(n1_ref[h, pl.ds(i, 1), :], (BF16_SUBLANES, tb)).astype(bf16)
                a_row = jnp.broadcast_to(a1_ref[h, pl.ds(i, 1), :], (BF16_SUBLANES, tb)).astype(bf16)
                n_all = jnp.tile(n_row, (reps, 1))
                a_all = jnp.tile(a_row, (reps, 1))
                g = g + jnp.where(r2_ref[h] < n_all, e2_ref[h] * a_all, jnp.zeros((), bf16))
            g_ref[slot, PEER_NKEYS * ii:PEER_NKEYS * (ii + 1), :] = g

    def hidden(slot):
        return jnp.dot(u_ref[PEER_EB * slot:PEER_EB * (slot + 1), :], xt_ref[...], preferred_element_type=f32)

    def outputs(slot, hh):
        act = 0.5 * hh * (1.0 + lax.erf(hh * (2.0 ** -0.5)))
        act_ref[slot] = (act * g_ref[slot].astype(f32)).astype(bf16)
        acc_ref[...] += jnp.dot(vt_ref[:, PEER_EB * slot:PEER_EB * (slot + 1)], act_ref[slot],
                                preferred_element_type=f32)

    @pl.when(e == 0)
    def _():
        for slot in range(1, PEER_NB):
            gates(slot, slot)

    gates(PEER_NB * e, 0)
    hts = [hidden(slot) for slot in range(PEER_NB)]
    for slot in range(PEER_NB):
        outputs(slot, hts[slot])
    for slot in range(1, PEER_NB):
        gates(PEER_NB * (e + 1) + slot, slot)

    @pl.when(e == pl.num_programs(1) - 1)
    def _():
        y = x1_ref[...] + acc_ref[...].T
        if final_norm:
            ms = jnp.mean(y * y, axis=-1, keepdims=True)
            y = y * lax.rsqrt(ms + EPS) * gf_ref[...]
        o_ref[...] = y


def _peer_dense(xt, u, vt, n1, a1, r2, e2, x1, gf, final_norm, tb=512):
    n = xt.shape[1]
    ne = u.shape[0]
    step = PEER_NB * PEER_EB
    big = lambda: pl.BlockSpec((PEER_HEADS, PEER_NKEYS, tb), lambda t, e: (0, 0, t))
    return pl.pallas_call(
        functools.partial(_peer_dense_kernel, final_norm=final_norm),
        out_shape=jax.ShapeDtypeStruct((n, D_MODEL), f32),
        grid=(n // tb, ne // step),
        in_specs=[pl.BlockSpec((D_MODEL, tb), lambda t, e: (0, t)),
                  pl.BlockSpec((step, D_MODEL), lambda t, e: (e, 0)),
                  pl.BlockSpec((D_MODEL, step), lambda t, e: (0, e)),
                  big(), big(), big(), big(),
                  pl.BlockSpec((tb, D_MODEL), lambda t, e: (t, 0)),
                  pl.BlockSpec((1, D_MODEL), lambda t, e: (0, 0))],
        out_specs=pl.BlockSpec((tb, D_MODEL), lambda t, e: (t, 0)),
        scratch_shapes=[pltpu.VMEM((D_MODEL, tb), f32), pltpu.VMEM((PEER_NB, PEER_EB, tb), bf16),
                        pltpu.VMEM((PEER_NB, PEER_EB, tb), bf16)],
        compiler_params=_cparams(("parallel", "arbitrary")),
        name="peer_dense",
    )(xt, u, vt, n1, a1, r2, e2, x1, gf)


def _pack_w_in(w):
    offs = np.concatenate([[0], np.cumsum(np.array(IN_SIZES))]).tolist()
    seg = lambda k: w[:, offs[k]:offs[k + 1]]
    hq, hf, hi, hog, aq, ak, av, iq, ik, iw, ga, gb = [seg(k) for k in range(12)]
    pad = jnp.zeros((w.shape[0], 64 - IDX_HEADS), w.dtype)
    return jnp.concatenate([hq, hf, hi, hog, aq, iq, ak, av, ik, iw, pad, ga, gb], axis=1).astype(bf16)


def _layer(x2, batch, seq, norm_mix, w_in, lb, hg_norm, ikg, ikb, bias_tiles, w_up_a, w_up_b, w_out,
           norm_ffn, peer_wq, peer_keys, peer_u, peer_v):
    proj = _inproj(x2, norm_mix.reshape(1, -1), _pack_w_in(w_in))
    iq, ak, ik, avt, wt = _dsa_prep(proj, ikg.reshape(1, -1), ikb.reshape(1, -1))
    ya = _hgrn2(proj, lb.reshape(1, -1), hg_norm.reshape(1, -1), batch, seq)
    yb = _dsa(proj, iq, wt, ik, ak, avt, bias_tiles, batch, seq)
    x1, xn, qp = _merge(x2, ya, yb, proj, w_up_a.astype(bf16), w_up_b.astype(bf16), w_out.astype(bf16),
                        norm_ffn.reshape(1, -1), peer_wq.astype(bf16))
    route = _peer_route(qp, peer_keys[0].astype(bf16), peer_keys[1].astype(bf16))
    return xn, route, x1


def kernel(x, norm_mix, w_in, hg_lb, hg_norm, idx_k_norm_g, idx_k_norm_b, rel_bias, w_up_a, w_up_b, w_out,
           norm_ffn, peer_wq, peer_keys, peer_u, peer_v, norm_final):
    batch, seq, _ = x.shape
    depth = w_in.shape[0]
    lb_all = jnp.cumsum(jax.nn.softmax(hg_lb.astype(f32), axis=0), axis=0)
    bias_tiles = _rel_bias_tiles(rel_bias)
    x2 = x.reshape(batch * seq, D_MODEL)
    gf = norm_final.reshape(1, -1)
    for l in range(depth):
        xn, route, x1 = _layer(
            x2, batch, seq, norm_mix[l], w_in[l], lb_all[l], hg_norm[l], idx_k_norm_g[l], idx_k_norm_b[l],
            bias_tiles, w_up_a[l], w_up_b[l], w_out[l], norm_ffn[l], peer_wq[l], peer_keys[l], peer_u[l], peer_v[l])
        x2 = _peer_dense(xn, peer_u[l].astype(bf16), peer_v[l].astype(bf16).T, *route, x1, gf,
                         final_norm=(l == depth - 1))
    return x2.reshape(batch, seq, D_MODEL)
```

```python
import functools
import math

import jax
import jax.numpy as jnp
import numpy as np
from jax import lax
from jax.experimental import pallas as pl
from jax.experimental.pallas import tpu as pltpu

f32 = jnp.float32
bf16 = jnp.bfloat16
i32 = jnp.int32

D_MODEL = 1024
EPS = 1e-6
HG_HEADS, HG_D = 4, 128
HG_W = HG_HEADS * HG_D
ATT_HEADS, ATT_DH = 8, 64
ATT_W = ATT_HEADS * ATT_DH
IDX_HEADS, IDX_DH = 4, 64
TOPK_MAX = 256
REL_BUCKETS, REL_MAX_DIST = 32, 128
PEER_HEADS, PEER_NKEYS, PEER_HALF, PEER_TOPK = 8, 128, 128, 16
PEER_EXPERTS = PEER_NKEYS * PEER_NKEYS
IN_SIZES = (HG_W, HG_W, HG_W, HG_W, ATT_W, ATT_DH, ATT_DH, IDX_HEADS * IDX_DH, IDX_DH, IDX_HEADS, D_MODEL, D_MODEL)

LANES = 128
BF16_SUBLANES = 16
VMEM_LIMIT = 56 * 1024 * 1024

COL_BLK = 512
PROJ_W = 10 * COL_BLK
CB_HQ, CB_HF, CB_HI, CB_HOG, CB_AQ, CB_IDX = 0, 1, 2, 3, 4, 5
CB_GA, CB_GB = 3, 4
NEG = -1e30


def _cparams(sem):
    return pltpu.CompilerParams(dimension_semantics=sem, vmem_limit_bytes=VMEM_LIMIT)


def _inproj_kernel(x_ref, g_ref, w_ref, o_ref, xn_ref):
    @pl.when(pl.program_id(1) == 0)
    def _():
        x = x_ref[...]
        ms = jnp.mean(x * x, axis=-1, keepdims=True)
        xn_ref[...] = (x * lax.rsqrt(ms + EPS) * g_ref[...]).astype(bf16)

    o_ref[...] = jnp.dot(xn_ref[...], w_ref[...], preferred_element_type=f32)


def _inproj(x2, g, w, tm=1024):
    n = x2.shape[0]
    return pl.pallas_call(
        _inproj_kernel,
        out_shape=jax.ShapeDtypeStruct((n, PROJ_W), f32),
        grid=(n // tm, PROJ_W // COL_BLK),
        in_specs=[
            pl.BlockSpec((tm, D_MODEL), lambda i, j: (i, 0)),
            pl.BlockSpec((1, D_MODEL), lambda i, j: (0, 0)),
            pl.BlockSpec((D_MODEL, COL_BLK), lambda i, j: (0, j)),
        ],
        out_specs=pl.BlockSpec((tm, COL_BLK), lambda i, j: (i, j)),
        scratch_shapes=[pltpu.VMEM((tm, D_MODEL), bf16)],
        compiler_params=_cparams(("parallel", "arbitrary")),
        name="inproj",
    )(x2, g, w)


DSA_T = 128
DSA_VROWS = ATT_DH + BF16_SUBLANES


def _dsa_prep_kernel(p_ref, g_ref, b_ref, iq_ref, ak_ref, ik_ref, avt_ref, wt_ref):
    t = DSA_T
    p = p_ref[...]
    iq_ref[...] = p[:, 0:256].astype(bf16)
    ak_ref[...] = p[:, 256:320].astype(bf16)
    ik = p[:, 384:448]
    mu = jnp.mean(ik, axis=-1, keepdims=True)
    var = jnp.mean(jnp.square(ik - mu), axis=-1, keepdims=True)
    y = (ik - mu) * lax.rsqrt(var + EPS) * g_ref[...] + b_ref[...]
    ik_ref[...] = y.astype(bf16)
    kv_t = p[:, 256:384].T
    extra = jnp.where(lax.broadcasted_iota(i32, (DSA_VROWS - ATT_DH, t), 0) == 0, 1.0, 0.0).astype(bf16)
    for c in range(p.shape[0] // t):
        avt_ref[c] = jnp.concatenate([kv_t[64:128, t * c:t * (c + 1)].astype(bf16), extra], axis=0)
    wt_ref[...] = p[:, 384:512].T[64:72, :] * (IDX_HEADS ** -0.5 * IDX_DH ** -0.5)


def _dsa_prep(proj, g, b, tm=1024):
    n = proj.shape[0]
    t = DSA_T
    row = lambda w: pl.BlockSpec((tm, w), lambda i: (i, 0))
    return pl.pallas_call(
        _dsa_prep_kernel,
        out_shape=(
            jax.ShapeDtypeStruct((n, 256), bf16),
            jax.ShapeDtypeStruct((n, 64), bf16),
            jax.ShapeDtypeStruct((n, 64), bf16),
            jax.ShapeDtypeStruct((n // t, DSA_VROWS, t), bf16),
            jax.ShapeDtypeStruct((8, n), f32),
        ),
        grid=(n // tm,),
        in_specs=[
            pl.BlockSpec((tm, COL_BLK), lambda i: (i, CB_IDX)),
            pl.BlockSpec((1, 64), lambda i: (0, 0)),
            pl.BlockSpec((1, 64), lambda i: (0, 0)),
        ],
        out_specs=(row(256), row(64), row(64),
                   pl.BlockSpec((tm // t, DSA_VROWS, t), lambda i: (i, 0, 0)),
                   pl.BlockSpec((8, tm), lambda i: (0, i))),
        compiler_params=_cparams(("parallel",)),
        name="dsa_prep",
    )(proj, g, b)


HG_CHUNK = 32


def _split3(a):
    a1 = a.astype(bf16)
    r1 = a - a1.astype(f32)
    a2 = r1.astype(bf16)
    r2 = r1 - a2.astype(f32)
    return a1, a2, r2.astype(bf16)


def _hgrn2_kernel(q_ref, f_ref, i_ref, og_ref, lb_ref, gn_ref, o_ref, st_ref, oacc_ref, *, ts):
    c = HG_CHUNK

    @pl.when(pl.program_id(1) == 0)
    def _():
        st_ref[...] = jnp.zeros_like(st_ref)

    r_io = lax.broadcasted_iota(i32, (c, c), 0)
    c_io = lax.broadcasted_iota(i32, (c, c), 1)
    causal = c_io <= r_io
    tri = causal.astype(bf16)
    mid = c // 2 - 1

    heads = range(HG_HEADS)
    lanes = [slice(HG_D * h, HG_D * (h + 1)) for h in heads]

    def chunk_pair(cp, carry):
        chains = [(pl.ds(pl.multiple_of((2 * cp + u) * c, c), c), h) for u in range(2) for h in heads]
        q, k, v, b = [], [], [], []
        for rows, h in chains:
            qr = q_ref[rows, lanes[h]]
            q.append(qr * jax.nn.sigmoid(qr))
            lb = lb_ref[:, lanes[h]]
            fg = lb + (1.0 - lb) * jax.nn.sigmoid(f_ref[rows, lanes[h]])
            k.append(1.0 - fg)
            v.append(i_ref[rows, lanes[h]].astype(bf16))
            l1, l2, l3 = _split3(jnp.log(fg))
            b.append(jnp.dot(tri, l1, preferred_element_type=f32)
                     + jnp.dot(tri, l2, preferred_element_type=f32)
                     + jnp.dot(tri, l3, preferred_element_type=f32))
        a = []
        for n in range(len(chains)):
            b_mid = b[n][mid:mid + 1, :]
            qt = (q[n] * jnp.exp(b[n] - b_mid)).astype(bf16)
            kt = (k[n] * jnp.exp(b_mid - b[n])).astype(bf16)
            a.append(lax.dot_general(qt, kt, NT_DIMS, preferred_element_type=f32))
        intra = [jnp.dot(jnp.where(causal, a[n], 0.0).astype(bf16), v[n], preferred_element_type=f32)
                 for n in range(len(chains))]
        for n, (rows, h) in enumerate(chains):
            b_last = b[n][c - 1:c, :]
            st = st_ref[h]
            qe = (q[n] * jnp.exp(b[n])).astype(bf16)
            inter = lax.dot_general(qe, st.astype(bf16), NT_DIMS, preferred_element_type=f32)
            oacc_ref[rows, lanes[h]] = inter + intra[n]
            ke = (k[n] * jnp.exp(b_last - b[n])).astype(bf16)
            upd = lax.dot_general(v[n], ke, (((0,), (0,)), ((), ())), preferred_element_type=f32)
            st_ref[h] = st * jnp.exp(b_last) + upd
        return carry

    lax.fori_loop(0, ts // (2 * c), chunk_pair, 0)

    for h in range(HG_HEADS):
        sl = slice(HG_D * h, HG_D * (h + 1))
        o = oacc_ref[:, sl]
        ms = jnp.mean(o * o, axis=-1, keepdims=True)
        og = og_ref[:, sl]
        y = o * lax.rsqrt(ms + EPS) * gn_ref[:, sl] * (og * jax.nn.sigmoid(og))
        o_ref[:, sl] = y.astype(o_ref.dtype)


def _hgrn2(proj, lb, gn, batch, seq, ts=256):
    n = proj.shape[0]
    nsb = seq // ts
    col = lambda cb: pl.BlockSpec((ts, COL_BLK), lambda b, s, cb=cb: (b * nsb + s, cb))
    return pl.pallas_call(
        functools.partial(_hgrn2_kernel, ts=ts),
        out_shape=jax.ShapeDtypeStruct((n, HG_W), bf16),
        grid=(batch, nsb),
        in_specs=[col(CB_HQ), col(CB_HF), col(CB_HI), col(CB_HOG),
                  pl.BlockSpec((1, HG_W), lambda b, s: (0, 0)),
                  pl.BlockSpec((1, HG_W), lambda b, s: (0, 0))],
        out_specs=pl.BlockSpec((ts, HG_W), lambda b, s: (b * nsb + s, 0)),
        scratch_shapes=[pltpu.VMEM((HG_HEADS, HG_D, HG_D), f32), pltpu.VMEM((ts, HG_W), f32)],
        compiler_params=_cparams(("parallel", "arbitrary")),
        name="hgrn2",
    )(proj, proj, proj, proj, lb, gn)


LOG2E = 1.4426950408889634
FLT_MAX = 3.4028234663852886e38
NT_DIMS = (((1,), (1,)), ((), ()))


def _key_to_float(key):
    bits = jnp.where(key >= 0, key, key ^ jnp.int32(0x7FFFFFFF))
    return pltpu.bitcast(bits, f32)


def _dsa_kernel(iq_ref, wt_ref, aq_ref, ik_ref, ak_ref, avt_ref, bias_ref, o_ref,
                sc_ref, iqs_ref, qs_ref, lg_ref, pt_ref, al_ref, acc_ref, bef_ref, *, ktop):
    t = DSA_T
    qi = pl.program_id(1)
    npair = (qi + 2) // 2
    krow = lax.broadcasted_iota(i32, (t, t), 0)
    qcol = lax.broadcasted_iota(i32, (t, t), 1)
    qpos = qi * t + lax.broadcasted_iota(i32, (1, t), 1)

    iq = iq_ref[...]
    for j in range(IDX_HEADS):
        iqs_ref[t * j:t * (j + 1), :] = iq[:, IDX_DH * j:IDX_DH * (j + 1)]
    aq = aq_ref[...] * (ATT_DH ** -0.5 * LOG2E)
    for h in range(ATT_HEADS):
        qs_ref[t * h:t * (h + 1), :] = aq[:, ATT_DH * h:ATT_DH * (h + 1)].astype(bf16)
    wt = wt_ref[...]

    def score_pair(kp, npos):
        ksl = pl.ds(pl.multiple_of(kp * 2 * t, 2 * t), 2 * t)
        rel = lax.dot_general(ik_ref[ksl, :], iqs_ref[...], NT_DIMS, preferred_element_type=f32)
        for u in range(2):
            kb = 2 * kp + u
            s = jnp.zeros((t, t), f32)
            for j in range(IDX_HEADS):
                s = s + wt[j:j + 1, :] * jnp.maximum(rel[t * u:t * (u + 1), t * j:t * (j + 1)], 0.0)
            s = jnp.where((kb * t + krow) <= (qi * t + qcol), s, -jnp.inf)
            sc_ref[kb] = s
            npos = npos + (s > 0.0).astype(i32)
        return npos

    def score_quad(kq, npos):
        return score_pair(2 * kq + 1, score_pair(2 * kq, npos))

    npos = lax.fori_loop(0, (qi + 4) // 4, score_quad, jnp.zeros((t, t), i32))

    def count_keys(pred):
        def body(kp, acc):
            acc = acc + pred(sc_ref[2 * kp], 2 * kp).astype(i32)
            return acc + pred(sc_ref[2 * kp + 1], 2 * kp + 1).astype(i32)
        acc = lax.fori_loop(0, npair, body, jnp.zeros((t, t), i32))
        return jnp.sum(acc, axis=0, keepdims=True)

    take_all = qpos < ktop
    few_pos = jnp.sum(npos, axis=0, keepdims=True) < ktop

    def unsettled(key, cnt):
        done = take_all | (cnt == ktop) | (few_pos & (key == 0))
        return jnp.min(done.astype(i32)) == 0

    bits_per_check = 4

    def bit_cond(c):
        return (c[0] < 32 // bits_per_check) & unsettled(c[1], c[2])

    def bit_step(c):
        g, key, cnt = c
        for b in range(bits_per_check):
            cand = key + jnp.left_shift(jnp.int32(1), 31 - b - bits_per_check * g)
            cf = _key_to_float(cand)
            cc = count_keys(lambda s, kb: s >= cf)
            ok = cc >= ktop
            key = jnp.where(ok, cand, key)
            cnt = jnp.where(ok, cc, cnt)
        return g + 1, key, cnt

    _, key, cnt = lax.while_loop(bit_cond, bit_step,
                                 (jnp.int32(0), jnp.full((1, t), -2 ** 31, i32), qpos + 1))
    thr = jnp.where(take_all, -FLT_MAX, _key_to_float(key))

    @pl.when(jnp.max(jnp.where(take_all, ktop, cnt)) > ktop)
    def _():
        def tally(kb, carry):
            gt, before = carry
            s = sc_ref[kb]
            bef_ref[kb] = before
            ties = jnp.sum(jnp.where(s == thr, 1.0, 0.0), axis=0, keepdims=True)
            return gt + jnp.where(s > thr, 1.0, 0.0), before + ties

        zero = jnp.zeros((1, t), f32)
        nquad = (qi + 4) // 4
        gt, _ = lax.fori_loop(0, 4 * nquad, tally, (jnp.zeros((t, t), f32), zero))
        need = ktop - jnp.sum(gt, axis=0, keepdims=True)
        tri = (qcol <= krow).astype(bf16)

        def drop(kb):
            s = sc_ref[kb]
            tie = s == thr
            run = bef_ref[kb] + jnp.dot(tri, jnp.where(tie, 1.0, 0.0).astype(bf16), preferred_element_type=f32)
            sc_ref[kb] = jnp.where(tie & (run > need), -jnp.inf, s)

        def drop_quad(kq, carry):
            for u in range(4):
                drop(4 * kq + u)
            return carry

        lax.fori_loop(0, nquad, drop_quad, 0)

    nq = sc_ref.shape[0]
    hw = ATT_HEADS * t
    acc_ref[...] = jnp.zeros(acc_ref.shape, f32)
    pt_ref[1] = jnp.zeros(pt_ref.shape[1:], bf16)
    al_ref[1] = jnp.ones(al_ref.shape[1:], f32)

    def logit_stage(j, par):
        for u in range(2):
            kb = jnp.minimum(2 * j + u, nq - 1)
            kblk = ak_ref[pl.ds(pl.multiple_of(kb * t, t), t), :]
            lg_ref[par, u] = lax.dot_general(kblk, qs_ref[...], NT_DIMS, preferred_element_type=f32)

    def softmax_stage(j, par, m_all):
        for u in range(2):
            kb = 2 * j + u
            madd = jnp.where(sc_ref[kb] >= thr, 0.0, NEG)
            near = jnp.clip(qi - kb, 0, 2)
            ms, als = [], []
            for h in range(ATT_HEADS):
                hs = slice(t * h, t * (h + 1))
                lg = lg_ref[par, u, :, hs] + (madd + bias_ref[near, h])
                m_old = m_all[:, hs]
                m_new = jnp.maximum(m_old, jnp.max(lg, axis=0, keepdims=True))
                als.append(jnp.exp2(m_old - m_new))
                ms.append(m_new)
                pt_ref[par, u, :, hs] = jnp.exp2(lg - m_new).astype(bf16)
            al_ref[par, u] = jnp.concatenate(als, axis=1)
            m_all = jnp.concatenate(ms, axis=1)
        return m_all

    def value_stage(j, par):
        for u in range(2):
            kb = jnp.maximum(2 * j + u, 0)
            pv = jnp.dot(avt_ref[kb], pt_ref[par, u], preferred_element_type=f32)
            acc_ref[...] = acc_ref[...] * al_ref[par, u] + pv

    def attn_trip(q, carry):
        for par in range(2):
            j = 2 * q + par
            value_stage(j - 1, 1 - par)
            carry = softmax_stage(j, par, carry)
            logit_stage(j + 1, 1 - par)
        return carry

    logit_stage(0, 0)
    m_all = lax.fori_loop(0, npair // 2, attn_trip, jnp.full((1, hw), NEG, f32))

    @pl.when(npair % 2 == 1)
    def _():
        j = 2 * (npair // 2)
        value_stage(j - 1, 1)
        softmax_stage(j, 0, m_all)
        value_stage(j, 0)

    @pl.when(npair % 2 == 0)
    def _():
        value_stage(2 * (npair // 2) - 1, 1)

    acc = acc_ref[...]
    out_t = acc[:ATT_DH] / acc[ATT_DH:ATT_DH + 1]
    out_t = jnp.concatenate([out_t[:, t * h:t * (h + 1)] for h in range(ATT_HEADS)], axis=0)
    o_ref[...] = out_t.T.astype(o_ref.dtype)


def _rel_bias_tiles(rel_bias):
    t = DSA_T
    ki = jnp.arange(t)[:, None]
    qj = jnp.arange(t)[None, :]
    far = rel_bias[REL_BUCKETS - 1].astype(f32)
    tiles = []
    for v in range(2):
        dist = jnp.maximum(v * t + qj - ki, 0)
        max_exact = REL_BUCKETS // 2
        nf = jnp.maximum(dist, 1).astype(f32)
        large = max_exact + (jnp.log(nf / max_exact) / math.log(REL_MAX_DIST / max_exact)
                             * (REL_BUCKETS - max_exact)).astype(i32)
        large = jnp.minimum(large, REL_BUCKETS - 1)
        bucket = jnp.where(dist < max_exact, dist, large)
        tiles.append(jnp.transpose((rel_bias[bucket].astype(f32) - far) * LOG2E, (2, 0, 1)))
    tiles.append(jnp.zeros_like(tiles[0]))
    return jnp.stack(tiles)


def _dsa(proj, iq, wt, ik, ak, avt, bias_tiles, batch, seq):
    n = proj.shape[0]
    t = DSA_T
    nq = seq // t
    assert nq % 4 == 0 and 2 * t >= REL_MAX_DIST
    ktop = min(TOPK_MAX, seq // 4)
    qrow = lambda width: pl.BlockSpec((t, width), lambda b, q: (b * nq + q, 0))
    kv = pl.BlockSpec((seq, 64), lambda b, q: (b, 0))
    hw = ATT_HEADS * t
    return pl.pallas_call(
        functools.partial(_dsa_kernel, ktop=ktop),
        out_shape=jax.ShapeDtypeStruct((n, ATT_W), bf16),
        grid=(batch, nq),
        in_specs=[qrow(256),
                  pl.BlockSpec((8, t), lambda b, q: (0, b * nq + q)),
                  pl.BlockSpec((t, COL_BLK), lambda b, q: (b * nq + q, CB_AQ)),
                  kv, kv,
                  pl.BlockSpec((nq, DSA_VROWS, t), lambda b, q: (b, 0, 0)),
                  pl.BlockSpec((3, ATT_HEADS, t, t), lambda b, q: (0, 0, 0, 0))],
        out_specs=qrow(ATT_W),
        scratch_shapes=[pltpu.VMEM((nq, t, t), f32),
                        pltpu.VMEM((IDX_HEADS * t, IDX_DH), bf16),
                        pltpu.VMEM((hw, ATT_DH), bf16),
                        pltpu.VMEM((2, 2, t, hw), f32),
                        pltpu.VMEM((2, 2, t, hw), bf16),
                        pltpu.VMEM((2, 2, 1, hw), f32),
                        pltpu.VMEM((DSA_VROWS, hw), f32),
                        pltpu.VMEM((nq, 1, t), f32)],
        compiler_params=_cparams(("parallel", "arbitrary")),
        name="dsa",
    )(iq, wt, proj, ik, ak, avt, bias_tiles)


def _merge_kernel(x_ref, ya_ref, yb_ref, ga_ref, gb_ref, wa_ref, wb_ref, wo_ref, g2_ref, wq_ref,
                  x1_ref, xnt_ref, qp_ref):
    ha = jnp.dot(ya_ref[...], wa_ref[...], preferred_element_type=f32)
    hb = jnp.dot(yb_ref[...], wb_ref[...], preferred_element_type=f32)
    h = jax.nn.sigmoid(ga_ref[...]) * ha + jax.nn.sigmoid(gb_ref[...]) * hb
    x1 = x_ref[...] + jnp.dot(h.astype(bf16), wo_ref[...], preferred_element_type=f32)
    x1_ref[...] = x1
    ms = jnp.mean(x1 * x1, axis=-1, keepdims=True)
    xn = x1 * lax.rsqrt(ms + EPS) * g2_ref[...]
    xnt_ref[...] = xn.T.astype(bf16)
    qp_ref[...] = jnp.dot(xn.astype(bf16), wq_ref[...], preferred_element_type=f32).astype(bf16)


def _merge(x2, ya, yb, proj, wa, wb, wo, g2, wq, tm=512):
    n = x2.shape[0]
    qw = wq.shape[1]
    row = lambda width: pl.BlockSpec((tm, width), lambda i: (i, 0))
    full = lambda a: pl.BlockSpec(a.shape, lambda i: (0, 0))
    return pl.pallas_call(
        _merge_kernel,
        out_shape=(jax.ShapeDtypeStruct((n, D_MODEL), f32),
                   jax.ShapeDtypeStruct((D_MODEL, n), bf16),
                   jax.ShapeDtypeStruct((n, qw), bf16)),
        grid=(n // tm,),
        in_specs=[row(D_MODEL), row(HG_W), row(ATT_W),
                  pl.BlockSpec((tm, D_MODEL), lambda i: (i, CB_GA)),
                  pl.BlockSpec((tm, D_MODEL), lambda i: (i, CB_GB)),
                  full(wa), full(wb), full(wo), full(g2), full(wq)],
        out_specs=(row(D_MODEL), pl.BlockSpec((D_MODEL, tm), lambda i: (0, i)), row(qw)),
        compiler_params=_cparams(("parallel",)),
        name="merge",
    )(x2, ya, yb, proj, proj, wa, wb, wo, g2, wq)


_PAIR_COUNTS = tuple(PEER_TOPK // (p + 1) for p in range(PEER_TOPK))
_NCAND = sum(_PAIR_COUNTS)
F32_SUBLANES = 8


def _exchange(v, i, j):
    v[i], v[j] = jnp.maximum(v[i], v[j]), jnp.minimum(v[i], v[j])


def _bitonic_sort(v):
    n = len(v)
    k = 2
    while k <= n:
        j = k // 2
        while j >= 1:
            for i in range(n):
                l = i ^ j
                if l > i:
                    if i & k:
                        _exchange(v, l, i)
                    else:
                        _exchange(v, i, l)
            j //= 2
        k *= 2


def _bitonic_merge(v):
    j = len(v) // 2
    while j >= 1:
        for i in range(len(v)):
            if i ^ j > i:
                _exchange(v, i, i ^ j)
        j //= 2


def _merge_sublanes(v, shifts):
    for shift in shifts:
        n = len(v)
        v = [jnp.maximum(v[g], pltpu.roll(v[n - 1 - g], shift, axis=0)) for g in range(n)]
        _bitonic_merge(v)
    return v


def _top_sorted(s):
    v = [s[F32_SUBLANES * g:F32_SUBLANES * (g + 1), :] for g in range(s.shape[0] // F32_SUBLANES)]
    _bitonic_sort(v)
    return _merge_sublanes(v[:PEER_TOPK], (4, 2, 1))


def _peer_route_kernel(qp_ref, k1_ref, k2_ref, n1_ref, a1_ref, r2_ref, e2_ref):
    tb = qp_ref.shape[0]
    sub = lax.broadcasted_iota(i32, (F32_SUBLANES, tb), 0)
    groups = PEER_NKEYS // F32_SUBLANES
    for h in range(PEER_HEADS):
        q1 = qp_ref[:, 2 * PEER_HALF * h:2 * PEER_HALF * h + PEER_HALF]
        q2 = qp_ref[:, 2 * PEER_HALF * h + PEER_HALF:2 * PEER_HALF * (h + 1)]
        s1 = lax.dot_general(k1_ref[h], q1, NT_DIMS, preferred_element_type=f32)
        s2 = lax.dot_general(k2_ref[h], q2, NT_DIMS, preferred_element_type=f32)
        a = _top_sorted(s1)
        b = _top_sorted(s2)
        nreg = pl.next_power_of_2(pl.cdiv(_NCAND, F32_SUBLANES))
        cand = [jnp.full((F32_SUBLANES, tb), -jnp.inf, f32) for _ in range(nreg)]
        r = 0
        for p in range(PEER_TOPK):
            for q in range(_PAIR_COUNTS[p]):
                cand[r // F32_SUBLANES] = jnp.where(sub == r % F32_SUBLANES, a[p] + b[q], cand[r // F32_SUBLANES])
                r += 1
        _bitonic_sort(cand)
        top = cand + [pltpu.roll(cand[len(cand) - 1 - g], 4, axis=0) for g in range(len(cand))]
        _bitonic_merge(top)
        top = _merge_sublanes(top, (2, 1))
        thr = top[PEER_TOPK - 1]
        mx = a[0] + b[0]
        z = jnp.zeros_like(mx)
        for tv in top:
            z = z + jnp.exp(tv - mx)
        s1g = s1.reshape(groups, F32_SUBLANES, tb)
        s2g = s2.reshape(groups, F32_SUBLANES, tb)
        n1 = jnp.zeros_like(s1g)
        r2 = jnp.zeros_like(s2g)
        for q in range(PEER_TOPK):
            n1 = n1 + jnp.where(s1g + b[q] >= thr, 1.0, 0.0)
            r2 = r2 + jnp.where(b[q] > s2g, 1.0, 0.0)
        n1_ref[h] = n1.reshape(PEER_NKEYS, tb)
        a1_ref[h] = (jnp.exp(s1g - a[0]) / z).reshape(PEER_NKEYS, tb)
        r2_ref[h] = r2.reshape(PEER_NKEYS, tb).astype(bf16)
        e2_ref[h] = jnp.exp(s2g - b[0]).reshape(PEER_NKEYS, tb).astype(bf16)


def _peer_route(qp, k1, k2, tb=128):
    n = qp.shape[0]
    big = lambda: pl.BlockSpec((PEER_HEADS, PEER_NKEYS, tb), lambda i: (0, 0, i))
    bigs = lambda dt: jax.ShapeDtypeStruct((PEER_HEADS, PEER_NKEYS, n), dt)
    return pl.pallas_call(
        _peer_route_kernel,
        out_shape=(bigs(f32), bigs(f32), bigs(bf16), bigs(bf16)),
        grid=(n // tb,),
        in_specs=[pl.BlockSpec((tb, qp.shape[1]), lambda i: (i, 0)),
                  pl.BlockSpec(k1.shape, lambda i: (0, 0, 0)),
                  pl.BlockSpec(k2.shape, lambda i: (0, 0, 0))],
        out_specs=(big(), big(), big(), big()),
        compiler_params=_cparams(("parallel",)),
        name="peer_route",
    )(qp, k1, k2)


PEER_EB = 512
PEER_NB = 4


def _peer_dense_kernel(xt_ref, u_ref, vt_ref, n1_ref, a1_ref, r2_ref, e2_ref, x1_ref, gf_ref,
                       o_ref, acc_ref, act_ref, g_ref, *, final_norm):
    e = pl.program_id(1)
    tb = xt_ref.shape[1]

    @pl.when(e == 0)
    def _():
        acc_ref[...] = jnp.zeros_like(acc_ref)

    nblk = PEER_EB // PEER_NKEYS
    reps = PEER_NKEYS // BF16_SUBLANES

    def gates(blk, slot):
        for ii in range(nblk):
            i = jnp.minimum(blk * nblk + ii, PEER_NKEYS - 1)
            g = jnp.zeros((PEER_NKEYS, tb), bf16)
            for h in range(PEER_HEADS):
                n_row = jnp.broadcast_to(n1_ref[h, pl.ds(i, 1), :], (BF16_SUBLANES, tb)).astype(bf16)
                a_row = jnp.broadcast_to(a1_ref[h, pl.ds(i, 1), :], (BF16_SUBLANES, tb)).astype(bf16)
                n_all = jnp.tile(n_row, (reps, 1))
                a_all = jnp.tile(a_row, (reps, 1))
                g = g + jnp.where(r2_ref[h] < n_all, e2_ref[h] * a_all, jnp.zeros((), bf16))
            g_ref[slot, PEER_NKEYS * ii:PEER_NKEYS * (ii + 1), :] = g

    def hidden(slot):
        return jnp.dot(u_ref[PEER_EB * slot:PEER_EB * (slot + 1), :], xt_ref[...], preferred_element_type=f32)

    def outputs(slot, hh):
        act = 0.5 * hh * (1.0 + lax.erf(hh * (2.0 ** -0.5)))
        act_ref[slot] = (act * g_ref[slot].astype(f32)).astype(bf16)
        acc_ref[...] += jnp.dot(vt_ref[:, PEER_EB * slot:PEER_EB * (slot + 1)], act_ref[slot],
                                preferred_element_type=f32)

    @pl.when(e == 0)
    def _():
        for slot in range(1, PEER_NB):
            gates(slot, slot)

    gates(PEER_NB * e, 0)
    hts = [hidden(slot) for slot in range(PEER_NB)]
    for slot in range(PEER_NB):
        outputs(slot, hts[slot])
    for slot in range(1, PEER_NB):
        gates(PEER_NB * (e + 1) + slot, slot)

    @pl.when(e == pl.num_programs(1) - 1)
    def _():
        y = x1_ref[...] + acc_ref[...].T
        if final_norm:
            ms = jnp.mean(y * y, axis=-1, keepdims=True)
            y = y * lax.rsqrt(ms + EPS) * gf_ref[...]
        o_ref[...] = y


def _peer_dense(xt, u, vt, n1, a1, r2, e2, x1, gf, final_norm, tb=512):
    n = xt.shape[1]
    ne = u.shape[0]
    step = PEER_NB * PEER_EB
    big = lambda: pl.BlockSpec((PEER_HEADS, PEER_NKEYS, tb), lambda t, e: (0, 0, t))
    return pl.pallas_call(
        functools.partial(_peer_dense_kernel, final_norm=final_norm),
        out_shape=jax.ShapeDtypeStruct((n, D_MODEL), f32),
        grid=(n // tb, ne // step),
        in_specs=[pl.BlockSpec((D_MODEL, tb), lambda t, e: (0, t)),
                  pl.BlockSpec((step, D_MODEL), lambda t, e: (e, 0)),
                  pl.BlockSpec((D_MODEL, step), lambda t, e: (0, e)),
                  big(), big(), big(), big(),
                  pl.BlockSpec((tb, D_MODEL), lambda t, e: (t, 0)),
                  pl.BlockSpec((1, D_MODEL), lambda t, e: (0, 0))],
        out_specs=pl.BlockSpec((tb, D_MODEL), lambda t, e: (t, 0)),
        scratch_shapes=[pltpu.VMEM((D_MODEL, tb), f32), pltpu.VMEM((PEER_NB, PEER_EB, tb), bf16),
                        pltpu.VMEM((PEER_NB, PEER_EB, tb), bf16)],
        compiler_params=_cparams(("parallel", "arbitrary")),
        name="peer_dense",
    )(xt, u, vt, n1, a1, r2, e2, x1, gf)


def _pack_w_in(w):
    offs = np.concatenate([[0], np.cumsum(np.array(IN_SIZES))]).tolist()
    seg = lambda k: w[:, offs[k]:offs[k + 1]]
    hq, hf, hi, hog, aq, ak, av, iq, ik, iw, ga, gb = [seg(k) for k in range(12)]
    pad = jnp.zeros((w.shape[0], 64 - IDX_HEADS), w.dtype)
    return jnp.concatenate([hq, hf, hi, hog, aq, iq, ak, av, ik, iw, pad, ga, gb], axis=1).astype(bf16)


def _layer(x2, batch, seq, norm_mix, w_in, lb, hg_norm, ikg, ikb, bias_tiles, w_up_a, w_up_b, w_out,
           norm_ffn, peer_wq, peer_keys, peer_u, peer_v):
    proj = _inproj(x2, norm_mix.reshape(1, -1), _pack_w_in(w_in))
    iq, ak, ik, avt, wt = _dsa_prep(proj, ikg.reshape(1, -1), ikb.reshape(1, -1))
    ya = _hgrn2(proj, lb.reshape(1, -1), hg_norm.reshape(1, -1), batch, seq)
    yb = _dsa(proj, iq, wt, ik, ak, avt, bias_tiles, batch, seq)
    x1, xn, qp = _merge(x2, ya, yb, proj, w_up_a.astype(bf16), w_up_b.astype(bf16), w_out.astype(bf16),
                        norm_ffn.reshape(1, -1), peer_wq.astype(bf16))
    route = _peer_route(qp, peer_keys[0].astype(bf16), peer_keys[1].astype(bf16))
    return xn, route, x1


def kernel(x, norm_mix, w_in, hg_lb, hg_norm, idx_k_norm_g, idx_k_norm_b, rel_bias, w_up_a, w_up_b, w_out,
           norm_ffn, peer_wq, peer_keys, peer_u, peer_v, norm_final):
    batch, seq, _ = x.shape
    depth = w_in.shape[0]
    lb_all = jnp.cumsum(jax.nn.softmax(hg_lb.astype(f32), axis=0), axis=0)
    bias_tiles = _rel_bias_tiles(rel_bias)
    x2 = x.reshape(batch * seq, D_MODEL)
    gf = norm_final.reshape(1, -1)
    for l in range(depth):
        xn, route, x1 = _layer(
            x2, batch, seq, norm_mix[l], w_in[l], lb_all[l], hg_norm[l], idx_k_norm_g[l], idx_k_norm_b[l],
            bias_tiles, w_up_a[l], w_up_b[l], w_out[l], norm_ffn[l], peer_wq[l], peer_keys[l], peer_u[l], peer_v[l])
        x2 = _peer_dense(xn, peer_u[l].astype(bf16), peer_v[l].astype(bf16).T, *route, x1, gf,
                         final_norm=(l == depth - 1))
    return x2.reshape(batch, seq, D_MODEL)
```
